```python
import math
import jax, jax.numpy as jnp
from jax import lax
import numpy as np

D_MODEL = 1024
BATCH = 8
SEQ = 4096
DEPTH = 4

N_MIXERS = 3
REL_BUCKETS = 32
REL_EXACT = REL_BUCKETS // 2
REL_MAX_DIST = 128
ATTN_HEADS = 16
HEAD_DIM = D_MODEL // ATTN_HEADS
NSA_KV_HEADS = 4
NSA_QPG = ATTN_HEADS // NSA_KV_HEADS
CMP_STRIDE = 16
CMP_LEN = 2 * CMP_STRIDE
CMP_HIDDEN = 2 * HEAD_DIM
SEL_BLOCK = 64
SEL_COUNT = 16
NSA_WINDOW = 512
NSA_QBLOCK = 64
NSA_IN = ATTN_HEADS * HEAD_DIM + 6 * NSA_KV_HEADS * HEAD_DIM + 3 * ATTN_HEADS
HGRN_HEADS = 8
HGRN_KDIM = D_MODEL // HGRN_HEADS
HGRN_VDIM = D_MODEL // HGRN_HEADS
HGRN_CHUNK = 64
SWA_KV_HEADS = 2
SWA_QPG = ATTN_HEADS // SWA_KV_HEADS
SWA_WINDOW = 128
SWA_BLOCK = SWA_WINDOW
SWA_IN = ATTN_HEADS * HEAD_DIM + 2 * SWA_KV_HEADS * HEAD_DIM
D_FF = 2816
LN_EPS = 1e-5
RMS_EPS = 1e-6
NEG_INF = -1e30
N_NSA = len(range(0, DEPTH, N_MIXERS))
N_HGRN = len(range(1, DEPTH, N_MIXERS))
N_SWA = len(range(2, DEPTH, N_MIXERS))

kernel_name = 'hybrid_nsa_hgrn2_swa_macaron_deepnorm'

F32 = jnp.float32


def _layer_norm(x, g, b):
    xf = x.astype(F32)
    mu = jnp.mean(xf, axis=-1, keepdims=True)
    xc = xf - mu
    var = jnp.mean(xc * xc, axis=-1, keepdims=True)
    return (xc * lax.rsqrt(var + LN_EPS) * g.astype(F32) + b.astype(F32)).astype(x.dtype)


def _swiglu(x, w_gate, w_up, w_down):
    return (jax.nn.silu(x @ w_gate) * (x @ w_up)) @ w_down


def _t5_bucket(dist):
    n = jnp.maximum(dist, 0)
    nf = jnp.maximum(n, 1).astype(F32)
    large = REL_EXACT + (jnp.log(nf / REL_EXACT) / math.log(REL_MAX_DIST / REL_EXACT)
                         * (REL_BUCKETS - REL_EXACT)).astype(jnp.int32)
    return jnp.where(n < REL_EXACT, n, jnp.minimum(large, REL_BUCKETS - 1))


def _rel_bias(table, dist):
    return jnp.moveaxis(table[_t5_bucket(dist)], -1, 0).astype(F32)


def _masked_softmax(logits, valid):
    p = jax.nn.softmax(jnp.where(valid, logits.astype(F32), NEG_INF), axis=-1)
    return jnp.where(valid, p, 0.0)


def _lower_bounds(lb_param):
    sm = jax.nn.softmax(lb_param.astype(F32), axis=0)
    return jnp.cumsum(sm, axis=0) - sm[0]


def _nsa(x, w_in, w_out, cmp_pos, cmp_w1, cmp_w2, rel_bias):
    B, T, _ = x.shape
    dt = x.dtype
    G, Q, Dh = NSA_KV_HEADS, NSA_QPG, HEAD_DIM
    n_cmp = T // CMP_STRIDE - 1
    n_blk = T // SEL_BLOCK
    n_sel = min(SEL_COUNT, n_blk)
    n_qb = T // NSA_QBLOCK
    scale = Dh ** -0.5
    kvw = G * Dh
    h = x @ w_in
    sizes = [ATTN_HEADS * Dh] + [kvw] * 6
    q, k_c, v_c, k_s, v_s, k_w, v_w, gate = jnp.split(h, np.cumsum(sizes).tolist(), axis=-1)

    def heads(z):
        return z.reshape(B, T, G, Dh)

    def compress(z, pos, w1, w2):
        ch = heads(z).reshape(B, T // CMP_STRIDE, CMP_STRIDE, G, Dh)
        blk = jnp.concatenate([ch[:, :-1], ch[:, 1:]], axis=2) + pos[:, None, :]
        hid = jax.nn.gelu(jnp.einsum('bnlgd,lde->bnge', blk, w1))
        return jnp.einsum('bnge,ed->bgnd', hid, w2)

    kc = compress(k_c, cmp_pos[0], cmp_w1[0], cmp_w2[0])
    vc = compress(v_c, cmp_pos[1], cmp_w1[1], cmp_w2[1])
    cmp_end = CMP_STRIDE * jnp.arange(n_cmp, dtype=jnp.int32) + CMP_LEN - 1
    ci = np.arange(n_cmp)[:, None]
    bj = np.arange(n_blk)[None, :]
    overlap = jnp.asarray(((CMP_STRIDE * ci < SEL_BLOCK * (bj + 1)) &
                           (CMP_STRIDE * ci + CMP_LEN > SEL_BLOCK * bj)).astype(np.float32))

    def sel_blocks(z):
        return heads(z).reshape(B, n_blk, SEL_BLOCK, G, Dh).transpose(0, 3, 1, 2, 4)

    ks, vs = sel_blocks(k_s), sel_blocks(v_s)

    def win_keys(z):
        return jnp.pad(heads(z).transpose(0, 2, 1, 3), ((0, 0), (0, 0), (NSA_WINDOW, 0), (0, 0)))

    kw, vw = win_keys(k_w), win_keys(v_w)
    q_blocks = q.reshape(B, n_qb, NSA_QBLOCK, G, Q, Dh).transpose(1, 0, 3, 4, 2, 5)
    g_blocks = jax.nn.sigmoid(gate.astype(F32)).reshape(B, n_qb, NSA_QBLOCK, G, Q, 3).transpose(1, 0, 3, 4, 2, 5)
    bi = jnp.arange(B)[:, None, None, None]
    gi = jnp.arange(G)[None, :, None, None]
    tab_g = rel_bias.reshape(REL_BUCKETS, G, Q).transpose(1, 0, 2)
    blk_ids = jnp.arange(n_blk, dtype=jnp.int32)
    tok_in_blk = jnp.arange(SEL_BLOCK, dtype=jnp.int32)
    win_off = jnp.arange(NSA_WINDOW + NSA_QBLOCK, dtype=jnp.int32)

    def block(args):
        qb, gb, qi = args
        t = qi * NSA_QBLOCK + jnp.arange(NSA_QBLOCK, dtype=jnp.int32)
        d_c = t[:, None] - cmp_end[None, :]
        l_c = (jnp.einsum('bgqtd,bgnd->bgqtn', qb, kc).astype(F32) * scale
               + _rel_bias(rel_bias, d_c).reshape(G, Q, NSA_QBLOCK, n_cmp))
        p_c = _masked_softmax(l_c, d_c >= 0)
        o_c = jnp.einsum('bgqtn,bgnd->bgqtd', p_c.astype(dt), vc)
        imp = jnp.einsum('bgqtn,nj->bgtj', p_c, overlap)
        cur = (t // SEL_BLOCK)[:, None]
        forced = (blk_ids == 0) | (blk_ids == cur) | (blk_ids == cur - 1)
        imp = jnp.where(blk_ids > cur, -1e9, jnp.where(forced, 1e9, imp))
        _, idx = lax.top_k(imp, n_sel)
        k_g = ks[bi, gi, idx].reshape(B, G, NSA_QBLOCK, n_sel * SEL_BLOCK, Dh)
        v_g = vs[bi, gi, idx].reshape(B, G, NSA_QBLOCK, n_sel * SEL_BLOCK, Dh)
        pos = (idx[..., None] * SEL_BLOCK + tok_in_blk).reshape(B, G, NSA_QBLOCK, n_sel * SEL_BLOCK)
        d_s = t[:, None] - pos
        b_s = tab_g[gi, _t5_bucket(d_s)].transpose(0, 1, 4, 2, 3).astype(F32)
        l_s = jnp.einsum('bgqtd,bgtsd->bgqts', qb, k_g).astype(F32) * scale + b_s
        p_s = _masked_softmax(l_s, (d_s >= 0)[:, :, None])
        o_s = jnp.einsum('bgqts,bgtsd->bgqtd', p_s.astype(dt), v_g)
        start = qi * NSA_QBLOCK
        k_wb = lax.dynamic_slice_in_dim(kw, start, NSA_WINDOW + NSA_QBLOCK, axis=2)
        v_wb = lax.dynamic_slice_in_dim(vw, start, NSA_WINDOW + NSA_QBLOCK, axis=2)
        s_abs = start - NSA_WINDOW + win_off
        d_w = t[:, None] - s_abs[None, :]
        valid_w = (d_w >= 0) & (d_w < NSA_WINDOW) & (s_abs >= 0)[None, :]
        l_w = (jnp.einsum('bgqtd,bgsd->bgqts', qb, k_wb).astype(F32) * scale
               + _rel_bias(rel_bias, d_w).reshape(G, Q, NSA_QBLOCK, NSA_WINDOW + NSA_QBLOCK))
        p_w = _masked_softmax(l_w, valid_w)
        o_w = jnp.einsum('bgqts,bgsd->bgqtd', p_w.astype(dt), v_wb)
        return (gb[..., 0:1] * o_c + gb[..., 1:2] * o_s + gb[..., 2:3] * o_w).astype(dt)

    o = lax.map(block, (q_blocks, g_blocks, jnp.arange(n_qb, dtype=jnp.int32)))
    o = o.transpose(1, 0, 4, 2, 3, 5).reshape(B, T, D_MODEL)
    return o @ w_out


def _hgrn2(x, w_in, w_out, norm_gain, lb):
    B, T, _ = x.shape
    H, K, V, C = HGRN_HEADS, HGRN_KDIM, HGRN_VDIM, HGRN_CHUNK
    n_ch = T // C
    h = (x @ w_in).astype(F32)
    zq, zf, zi, zg = jnp.split(h, 4, axis=-1)
    lbf = lb.astype(F32)
    q = jax.nn.silu(zq)
    log_f = jnp.logaddexp(jnp.log(lbf), jnp.log1p(-lbf) + jax.nn.log_sigmoid(zf))
    k = (1.0 - lbf) * jax.nn.sigmoid(-zf)

    def chunks(z, d):
        return z.reshape(B, n_ch, C, H, d).transpose(1, 0, 3, 2, 4)

    causal = jnp.tril(jnp.ones((C, C), dtype=bool))[None, None, :, :, None]

    def step(S, inp):
        qc, kc, vc, gc = inp
        bcum = jnp.cumsum(gc, axis=2)
        o_inter = jnp.einsum('bhck,bhkv->bhcv', qc * jnp.exp(bcum), S)
        diff = jnp.where(causal, bcum[:, :, :, None, :] - bcum[:, :, None, :, :], -jnp.inf)
        a = jnp.einsum('bhtk,bhsk,bhtsk->bhts', qc, kc, jnp.exp(diff))
        o_intra = jnp.einsum('bhts,bhsv->bhtv', a, vc)
        b_last = bcum[:, :, -1]
        S_new = (jnp.exp(b_last)[..., None] * S
                 + jnp.einsum('bhsk,bhsv->bhkv', kc * jnp.exp(b_last[:, :, None] - bcum), vc))
        return S_new, o_inter + o_intra

    S0 = jnp.zeros((B, H, K, V), F32)
    _, o = lax.scan(step, S0, (chunks(q, K), chunks(k, K), chunks(zi, V), chunks(log_f, K)))
    o = o.transpose(1, 0, 3, 2, 4).reshape(B, T, H, V)
    o = o * lax.rsqrt(jnp.mean(o * o, axis=-1, keepdims=True) + RMS_EPS) * norm_gain.astype(F32)
    o = o * jax.nn.silu(zg.reshape(B, T, H, V))
    return o.reshape(B, T, D_MODEL).astype(x.dtype) @ w_out


def _swa(x, w_in, w_out, sinks, rel_bias):
    B, T, _ = x.shape
    nb = T // SWA_BLOCK
    Kv, Q, Dh, L = SWA_KV_HEADS, SWA_QPG, HEAD_DIM, SWA_BLOCK
    scale = Dh ** -0.5
    h = x @ w_in
    q, k, v = jnp.split(h, [ATTN_HEADS * Dh, ATTN_HEADS * Dh + Kv * Dh], axis=-1)
    q = q.reshape(B, nb, L, Kv, Q, Dh)

    def band(z):
        z = jnp.pad(z.reshape(B, T, Kv, Dh), ((0, 0), (L, 0), (0, 0), (0, 0))).reshape(B, nb + 1, L, Kv, Dh)
        return jnp.concatenate([z[:, :-1], z[:, 1:]], axis=2)

    kb, vb = band(k), band(v)
    tl = jnp.arange(L, dtype=jnp.int32)
    sl = jnp.arange(2 * L, dtype=jnp.int32)
    dist = tl[:, None] + L - sl[None, :]
    bias = _rel_bias(rel_bias, dist).reshape(Kv, Q, L, 2 * L)
    abs_s = jnp.arange(nb, dtype=jnp.int32)[:, None] * L - L + sl[None, :]
    valid = ((dist >= 0) & (dist < SWA_WINDOW))[None] & (abs_s >= 0)[:, None, :]
    logits = jnp.einsum('bntgqd,bnsgd->bngqts', q, kb).astype(F32) * scale + bias
    logits = jnp.where(valid[None, :, None, None], logits, NEG_INF)
    sink = sinks.astype(F32).reshape(Kv, Q)[None, None, :, :, None, None]
    m = jnp.maximum(jnp.max(logits, axis=-1, keepdims=True), sink)
    e = jnp.exp(logits - m)
    p = e / (jnp.sum(e, axis=-1, keepdims=True) + jnp.exp(sink - m))
    o = jnp.einsum('bngqts,bnsgd->bntgqd', p.astype(x.dtype), vb).reshape(B, T, D_MODEL)
    return o @ w_out


def setup_inputs(seed: int = 0) -> dict:
    key = jax.random.key(seed)
    ks = jax.random.split(key, 22)
    beta = (8.0 * DEPTH) ** -0.25
    D = D_MODEL

    def nrm(k, shape, scale):
        return jax.random.normal(k, shape, F32) * scale

    return {
        'x': nrm(ks[0], (BATCH, SEQ, D), 1.0),
        'rel_bias': nrm(ks[1], (REL_BUCKETS, ATTN_HEADS), 0.5),
        'ln_gain': 1.0 + nrm(ks[2], (DEPTH, 3, D), 0.02),
        'ln_bias': nrm(ks[3], (DEPTH, 3, D), 0.02),
        'ffn1_w_gate': nrm(ks[4], (DEPTH, D, D_FF), D ** -0.5),
        'ffn1_w_up': nrm(ks[5], (DEPTH, D, D_FF), D ** -0.5),
        'ffn1_w_down': nrm(ks[6], (DEPTH, D_FF, D), beta * D_FF ** -0.5),
        'ffn2_w_gate': nrm(ks[7], (DEPTH, D, D_FF), D ** -0.5),
        'ffn2_w_up': nrm(ks[8], (DEPTH, D, D_FF), D ** -0.5),
        'ffn2_w_down': nrm(ks[9], (DEPTH, D_FF, D), beta * D_FF ** -0.5),
        'nsa_w_in': nrm(ks[10], (N_NSA, D, NSA_IN), D ** -0.5),
        'nsa_w_out': nrm(ks[11], (N_NSA, D, D), beta * D ** -0.5),
        'nsa_cmp_pos': nrm(ks[12], (N_NSA, 2, CMP_LEN, HEAD_DIM), 0.1),
        'nsa_cmp_w1': nrm(ks[13], (N_NSA, 2, CMP_LEN, HEAD_DIM, CMP_HIDDEN), (CMP_LEN * HEAD_DIM) ** -0.5),
        'nsa_cmp_w2': nrm(ks[14], (N_NSA, 2, CMP_HIDDEN, HEAD_DIM), CMP_HIDDEN ** -0.5),
        'hgrn_w_in': nrm(ks[15], (N_HGRN, D, 4 * D), D ** -0.5),
        'hgrn_w_out': nrm(ks[16], (N_HGRN, D, D), beta * D ** -0.5),
        'hgrn_norm_gain': 1.0 + nrm(ks[17], (N_HGRN, HGRN_VDIM), 0.02),
        'hgrn_lb': nrm(ks[18], (DEPTH, D), 1.0),
        'swa_w_in': nrm(ks[19], (N_SWA, D, SWA_IN), D ** -0.5),
        'swa_w_out': nrm(ks[20], (N_SWA, D, D), beta * D ** -0.5),
        'swa_sinks': nrm(ks[21], (N_SWA, ATTN_HEADS), 0.5),
    }


def reference(x, rel_bias, ln_gain, ln_bias, ffn1_w_gate, ffn1_w_up, ffn1_w_down,
              ffn2_w_gate, ffn2_w_up, ffn2_w_down, nsa_w_in, nsa_w_out, nsa_cmp_pos,
              nsa_cmp_w1, nsa_cmp_w2, hgrn_w_in, hgrn_w_out, hgrn_norm_gain, hgrn_lb,
              swa_w_in, swa_w_out, swa_sinks):
    alpha = (2.0 * DEPTH) ** 0.25
    lbs = _lower_bounds(hgrn_lb)
    for i in range(DEPTH):
        x = _layer_norm(alpha * x + 0.5 * _swiglu(x, ffn1_w_gate[i], ffn1_w_up[i], ffn1_w_down[i]),
                        ln_gain[i, 0], ln_bias[i, 0])
        kind, slot = i % N_MIXERS, i // N_MIXERS
        if kind == 0:
            y = _nsa(x, nsa_w_in[slot], nsa_w_out[slot], nsa_cmp_pos[slot], nsa_cmp_w1[slot],
                     nsa_cmp_w2[slot], rel_bias)
        elif kind == 1:
            y = _hgrn2(x, hgrn_w_in[slot], hgrn_w_out[slot], hgrn_norm_gain[slot], lbs[i])
        else:
            y = _swa(x, swa_w_in[slot], swa_w_out[slot], swa_sinks[slot], rel_bias)
        x = _layer_norm(alpha * x + y, ln_gain[i, 1], ln_bias[i, 1])
        x = _layer_norm(alpha * x + 0.5 * _swiglu(x, ffn2_w_gate[i], ffn2_w_up[i], ffn2_w_down[i]),
                        ln_gain[i, 2], ln_bias[i, 2])
    return x
```

```python
import functools
import math

import jax
import jax.numpy as jnp
from jax import lax
from jax.experimental import pallas as pl
from jax.experimental.pallas import tpu as pltpu

F32 = jnp.float32
BF16 = jnp.bfloat16

DEPTH = 4
N_MIXERS = 3
REL_BUCKETS = 32
REL_EXACT = REL_BUCKETS // 2
REL_MAX_DIST = 128
ATTN_HEADS = 16
HEAD_DIM = 64
NSA_KV_HEADS = 4
NSA_QPG = ATTN_HEADS // NSA_KV_HEADS
CMP_STRIDE = 16
CMP_LEN = 2 * CMP_STRIDE
SEL_BLOCK = 64
SEL_COUNT = 16
NSA_WINDOW = 512
HGRN_HEADS = 8
HGRN_CHUNK = 64
HGRN_SUB = 16
SWA_KV_HEADS = 2
SWA_QPG = ATTN_HEADS // SWA_KV_HEADS
SWA_WINDOW = 128
LN_EPS = 1e-5
RMS_EPS = 1e-6
NEG_INF = -1e30

LANE = 128
VMEM_LIMIT = 48 * 1024 * 1024
NSA_TQ = 128
NSA_NEAR = NSA_TQ // CMP_STRIDE + 8


def _cparams(sem):
    return pltpu.CompilerParams(dimension_semantics=sem, vmem_limit_bytes=VMEM_LIMIT)


def _dot(a, b):
    return jnp.dot(a, b, preferred_element_type=F32)


def _dot_nt(a, b):
    return lax.dot_general(a, b, (((1,), (1,)), ((), ())), preferred_element_type=F32)


def _layer_norm_rows(y, g, b):
    mu = jnp.mean(y, axis=-1, keepdims=True)
    yc = y - mu
    var = jnp.mean(yc * yc, axis=-1, keepdims=True)
    return yc * lax.rsqrt(var + LN_EPS) * g + b


def _ffn_ln_kernel(x_ref, wg_ref, wu_ref, wd_ref, g_ref, b_ref, o_ref, xb_ref, acc_ref, *, alpha):
    j = pl.program_id(1)

    @pl.when(j == 0)
    def _():
        xb_ref[...] = x_ref[...].astype(BF16)
        acc_ref[...] = jnp.zeros_like(acc_ref)

    xb = xb_ref[...]
    gate = _dot(xb, wg_ref[...])
    up = _dot(xb, wu_ref[...])
    h = (jax.nn.silu(gate) * up).astype(BF16)
    acc_ref[...] += _dot(h, wd_ref[...])

    @pl.when(j == pl.num_programs(1) - 1)
    def _():
        y = alpha * x_ref[...] + 0.5 * acc_ref[...]
        o_ref[...] = _layer_norm_rows(y, g_ref[...], b_ref[...])


def _ffn_ln(x, wg, wu, wd, gain, bias, alpha, tm=512, tf=1408):
    n, d = x.shape
    f = wg.shape[1]
    tm = min(tm, n)
    assert n % tm == 0 and f % tf == 0
    return pl.pallas_call(
        functools.partial(_ffn_ln_kernel, alpha=alpha),
        grid=(n // tm, f // tf),
        in_specs=[
            pl.BlockSpec((tm, d), lambda i, j: (i, 0)),
            pl.BlockSpec((d, tf), lambda i, j: (0, j)),
            pl.BlockSpec((d, tf), lambda i, j: (0, j)),
            pl.BlockSpec((tf, d), lambda i, j: (j, 0)),
            pl.BlockSpec((1, d), lambda i, j: (0, 0)),
            pl.BlockSpec((1, d), lambda i, j: (0, 0)),
        ],
        out_specs=pl.BlockSpec((tm, d), lambda i, j: (i, 0)),
        out_shape=jax.ShapeDtypeStruct((n, d), F32),
        scratch_shapes=[pltpu.VMEM((tm, d), BF16), pltpu.VMEM((tm, d), F32)],
        compiler_params=_cparams(("parallel", "arbitrary")),
        name="ffn_ln",
    )(x, wg, wu, wd, gain.reshape(1, d), bias.reshape(1, d))


def _proj_kernel(x_ref, w_ref, o_ref):
    o_ref[...] = _dot(x_ref[...].astype(BF16), w_ref[...]).astype(o_ref.dtype)


def _proj(x, w, tn, tm=1024, out_dtype=F32):
    n, k = x.shape
    m = w.shape[1]
    tm = min(tm, n)
    assert n % tm == 0 and m % tn == 0
    return pl.pallas_call(
        _proj_kernel,
        grid=(n // tm, m // tn),
        in_specs=[pl.BlockSpec((tm, k), lambda i, j: (i, 0)),
                  pl.BlockSpec((k, tn), lambda i, j: (0, j))],
        out_specs=pl.BlockSpec((tm, tn), lambda i, j: (i, j)),
        out_shape=jax.ShapeDtypeStruct((n, m), out_dtype),
        compiler_params=_cparams(("parallel", "arbitrary")),
        name="proj",
    )(x, w)


def _proj_ln_kernel(a_ref, w_ref, r_ref, g_ref, b_ref, o_ref, *, alpha):
    y = _dot(a_ref[...].astype(BF16), w_ref[...])
    o_ref[...] = _layer_norm_rows(alpha * r_ref[...] + y, g_ref[...], b_ref[...])


def _proj_ln(a, w, res, gain, bias, alpha, tm=512):
    n, k = a.shape
    d = w.shape[1]
    tm = min(tm, n)
    assert n % tm == 0
    return pl.pallas_call(
        functools.partial(_proj_ln_kernel, alpha=alpha),
        grid=(n // tm,),
        in_specs=[pl.BlockSpec((tm, k), lambda i: (i, 0)),
                  pl.BlockSpec((k, d), lambda i: (0, 0)),
                  pl.BlockSpec((tm, d), lambda i: (i, 0)),
                  pl.BlockSpec((1, d), lambda i: (0, 0)),
                  pl.BlockSpec((1, d), lambda i: (0, 0))],
        out_specs=pl.BlockSpec((tm, d), lambda i: (i, 0)),
        out_shape=jax.ShapeDtypeStruct((n, d), F32),
        compiler_params=_cparams(("parallel",)),
        name="proj_ln",
    )(a, w, res, gain.reshape(1, d), bias.reshape(1, d))


def _t5_bucket(dist):
    n = jnp.maximum(dist, 0)
    nf = jnp.maximum(n, 1).astype(F32)
    large = REL_EXACT + (jnp.log(nf / REL_EXACT) / math.log(REL_MAX_DIST / REL_EXACT)
                         * (REL_BUCKETS - REL_EXACT)).astype(jnp.int32)
    return jnp.where(n < REL_EXACT, n, jnp.minimum(large, REL_BUCKETS - 1))


def _bias_lookup(bucket, tab_ref, col):
    out = jnp.zeros(bucket.shape, F32)
    for k in range(REL_BUCKETS):
        out = jnp.where(bucket == k, tab_ref[k, col], out)
    return out


def _per_head_scalar(n_heads, fn):
    hidx = lax.broadcasted_iota(jnp.int32, (n_heads, 1, 1), 0)
    out = jnp.zeros((n_heads, 1, 1), F32)
    for h in range(n_heads):
        out = jnp.where(hidx == h, fn(h), out)
    return out


def _split3(x):
    hi = x.astype(BF16)
    r1 = x - hi.astype(F32)
    mid = r1.astype(BF16)
    lo = (r1 - mid.astype(F32)).astype(BF16)
    return hi, mid, lo


def _nsa_compress_kernel(zk_ref, zv_ref, pos_ref, w1_ref, w2_ref, kc_ref, vc_ref):
    for idx, (z_ref, o_ref) in enumerate(((zk_ref, kc_ref), (zv_ref, vc_ref))):
        z = z_ref[0, 0]
        n_chunk = z.shape[0]
        u = _dot((z + pos_ref[idx, 0]).astype(BF16), w1_ref[idx, 0])
        v = _dot((z + pos_ref[idx, 1]).astype(BF16), w1_ref[idx, 1])
        hid = u + pltpu.roll(v, n_chunk - 1, 0)
        act = jax.nn.gelu(hid)
        o_ref[0, 0] = _dot(act.astype(BF16), w2_ref[idx]).astype(o_ref.dtype)


def _nsa_compress(zk, zv, pos, w1, w2):
    b, g, nc, kd = zk.shape
    hid = w1.shape[-1]
    dh = w2.shape[-1]
    zspec = pl.BlockSpec((1, 1, nc, kd), lambda i, j: (i, j, 0, 0))
    ospec = pl.BlockSpec((1, 1, nc, dh), lambda i, j: (i, j, 0, 0))
    return pl.pallas_call(
        _nsa_compress_kernel,
        grid=(b, g),
        in_specs=[zspec, zspec,
                  pl.BlockSpec((2, 2, 1, kd), lambda i, j: (0, 0, 0, 0)),
                  pl.BlockSpec((2, 2, kd, hid), lambda i, j: (0, 0, 0, 0)),
                  pl.BlockSpec((2, hid, dh), lambda i, j: (0, 0, 0))],
        out_specs=[ospec, ospec],
        out_shape=[jax.ShapeDtypeStruct((b, g, nc, dh), BF16)] * 2,
        compiler_params=_cparams(("parallel", "parallel")),
        name="nsa_compress",
    )(zk, zv, pos, w1, w2)


def _nsa_attn_kernel(tab_ref, q_ref, kc_ref, vc_ref, ks_ref, vs_ref, kw_ref, vw_ref, gate_ref, o_ref,
                     bd_ref, bp_ref, cb_ref, vt_ref, sel_ref, m_ref, l_ref, acc_ref, *, n_sel):
    tq = NSA_TQ
    nq = NSA_QPG
    dh = HEAD_DIM
    g = pl.program_id(1)
    iq = pl.program_id(2)
    t0 = iq * tq
    n_cmp = kc_ref.shape[2]
    n_blk = vt_ref.shape[0]

    ii = lax.broadcasted_iota(jnp.int32, (tq, tq), 0)
    jj = lax.broadcasted_iota(jnp.int32, (tq, tq), 1)

    @pl.when(iq == 0)
    def _init():
        b_diag = _t5_bucket(ii - jj)
        b_prev = _t5_bucket(ii - jj + tq)
        i2 = lax.broadcasted_iota(jnp.int32, (tq, LANE), 0)
        mm = lax.broadcasted_iota(jnp.int32, (tq, LANE), 1)
        b_near = _t5_bucket(i2 - CMP_STRIDE * mm + (9 * CMP_STRIDE - CMP_LEN + 1))
        for h in range(nq):
            col = g * nq + h
            bd_ref[h] = _bias_lookup(b_diag, tab_ref, col)
            bp_ref[h] = _bias_lookup(b_prev, tab_ref, col)
            near = _bias_lookup(b_near, tab_ref, col)
            cbv = jnp.where(mm < NSA_NEAR, near,
                            jnp.where(mm == NSA_NEAR, tab_ref[REL_BUCKETS - 1, col], 0.0))
            hi, mid, lo = _split3(cbv)
            cb_ref[0, h] = hi
            cb_ref[1, h] = mid
            cb_ref[2, h] = lo

    far_bias = _per_head_scalar(nq, lambda h: tab_ref[REL_BUCKETS - 1, g * nq + h])

    qs = (q_ref[0].astype(F32) * (dh ** -0.5)).astype(BF16).reshape(nq * tq, dh)

    lc = _dot_nt(qs, kc_ref[0, 0])
    n_lo = t0 // CMP_STRIDE - 9
    mi = lax.broadcasted_iota(jnp.int32, (LANE, n_cmp), 0)
    ni = lax.broadcasted_iota(jnp.int32, (LANE, n_cmp), 1)
    place = ((mi < NSA_NEAR) & (ni == mi + n_lo)) | ((mi == NSA_NEAR) & (ni < n_lo))
    shift = jnp.where(place, 1.0, 0.0).astype(BF16)
    bias_c = (_dot(cb_ref[0].reshape(nq * tq, LANE), shift)
              + _dot(cb_ref[1].reshape(nq * tq, LANE), shift)
              + _dot(cb_ref[2].reshape(nq * tq, LANE), shift))
    lc3 = (lc + bias_c).reshape(nq, tq, n_cmp)
    tc = t0 + lax.broadcasted_iota(jnp.int32, (tq, n_cmp), 0)
    nc = lax.broadcasted_iota(jnp.int32, (tq, n_cmp), 1)
    valid_c = (tc - CMP_STRIDE * nc - (CMP_LEN - 1)) >= 0
    lcm = jnp.where(valid_c[None], lc3, NEG_INF)
    mx = jnp.max(lcm, axis=-1, keepdims=True)
    ec = jnp.where(valid_c[None], jnp.exp(lcm - mx), 0.0)
    sc = jnp.sum(ec, axis=-1, keepdims=True)
    pc = ec / jnp.where(sc > 0.0, sc, 1.0)
    o_c = _dot(pc.reshape(nq * tq, n_cmp).astype(BF16), vc_ref[0, 0]).reshape(nq, tq, dh)

    psum = pc[0]
    for h in range(1, nq):
        psum = psum + pc[h]
    p_hi = psum.astype(BF16)
    p_lo = (psum - p_hi.astype(F32)).astype(BF16)
    jb = lax.broadcasted_iota(jnp.int32, (n_blk, n_cmp), 0)
    nb = lax.broadcasted_iota(jnp.int32, (n_blk, n_cmp), 1)
    ovl = jnp.where((CMP_STRIDE * nb < SEL_BLOCK * (jb + 1)) & (CMP_STRIDE * nb + CMP_LEN > SEL_BLOCK * jb),
                    1.0, 0.0).astype(BF16)
    imp_t = _dot_nt(ovl, p_hi) + _dot_nt(ovl, p_lo)
    jb2 = lax.broadcasted_iota(jnp.int32, (n_blk, tq), 0)
    cur = (t0 + lax.broadcasted_iota(jnp.int32, (n_blk, tq), 1)) // SEL_BLOCK
    forced = (jb2 == 0) | (jb2 == cur) | (jb2 == cur - 1)
    val = jnp.where(jb2 > cur, -1e9, jnp.where(forced, 1e9, imp_t))
    vt_ref[...] = val

    def rank_body(jp, rank):
        row = vt_ref[pl.ds(jp, 1), :]
        beats = (row > val) | ((row == val) & (jp < jb2))
        return rank + jnp.where(beats, 1, 0)

    rank = lax.fori_loop(0, (t0 + tq) // SEL_BLOCK, rank_body, jnp.zeros((n_blk, tq), jnp.int32))
    sel_t = jnp.where(rank < n_sel, 1.0, 0.0)
    sel_ref[...] = sel_t.T.astype(BF16)

    def sel_mask(jt):
        eb = lax.broadcasted_iota(jnp.int32, (n_blk, tq), 0)
        es = lax.broadcasted_iota(jnp.int32, (n_blk, tq), 1)
        expand = jnp.where(eb == jt * (tq // SEL_BLOCK) + es // SEL_BLOCK, 1.0, 0.0).astype(BF16)
        return _dot(sel_ref[...], expand) > 0.5

    def flash_init():
        m_ref[...] = jnp.full(m_ref.shape, NEG_INF, F32)
        l_ref[...] = jnp.zeros(l_ref.shape, F32)
        acc_ref[...] = jnp.zeros(acc_ref.shape, F32)

    def flash_tile(k, v, bias, mask):
        s = _dot_nt(qs, k).reshape(nq, tq, tq) + bias
        if mask is not None:
            s = jnp.where(mask[None], s, NEG_INF)
        m_old = m_ref[...]
        m_new = jnp.maximum(m_old, jnp.max(s, axis=-1, keepdims=True))
        p = jnp.exp(s - m_new)
        if mask is not None:
            p = jnp.where(mask[None], p, 0.0)
        a = jnp.exp(m_old - m_new)
        l_ref[...] = a * l_ref[...] + jnp.sum(p, axis=-1, keepdims=True)
        pv = _dot(p.reshape(nq * tq, tq).astype(BF16), v).reshape(nq, tq, dh)
        acc_ref[...] = a * acc_ref[...] + pv
        m_ref[...] = m_new

    def kv_tile(k_ref, v_ref, jt):
        start = pl.multiple_of(jt * tq, tq)
        return k_ref[0, 0, pl.ds(start, tq), :], v_ref[0, 0, pl.ds(start, tq), :]

    causal = ii >= jj

    flash_init()

    def far_body(jt, carry):
        k, v = kv_tile(ks_ref, vs_ref, jt)
        flash_tile(k, v, far_bias, sel_mask(jt))
        return carry

    lax.fori_loop(0, jnp.maximum(iq - 1, 0), far_body, 0)

    @pl.when(iq >= 1)
    def _():
        k, v = kv_tile(ks_ref, vs_ref, iq - 1)
        flash_tile(k, v, bp_ref[...], sel_mask(iq - 1))

    k, v = kv_tile(ks_ref, vs_ref, iq)
    flash_tile(k, v, bd_ref[...], sel_mask(iq) & causal)
    o_s = acc_ref[...] / l_ref[...]

    flash_init()
    n_win = NSA_WINDOW // tq
    for r in range(n_win):
        jt = iq - (n_win - r)

        @pl.when(jt >= 0)
        def _(r=r, jt=jt):
            k, v = kv_tile(kw_ref, vw_ref, jt)
            if r == 0:
                flash_tile(k, v, far_bias, jj > ii)
            elif r == n_win - 1:
                flash_tile(k, v, bp_ref[...], None)
            else:
                flash_tile(k, v, far_bias, None)

    k, v = kv_tile(kw_ref, vw_ref, iq)
    flash_tile(k, v, bd_ref[...], causal)
    o_w = acc_ref[...] / l_ref[...]

    sg = jax.nn.sigmoid(gate_ref[0, 0]).T
    outs = []
    for h in range(nq):
        outs.append(sg[:, 3 * h:3 * h + 1] * o_c[h] + sg[:, 3 * h + 1:3 * h + 2] * o_s[h]
                    + sg[:, 3 * h + 2:3 * h + 3] * o_w[h])
    o_ref[0] = jnp.concatenate(outs, axis=1).astype(o_ref.dtype)


def _nsa_attention(tab, q, kc, vc, ks, vs, kw, vw, gate):
    b, _, t, dh = q.shape
    g = NSA_KV_HEADS
    nq = NSA_QPG
    tq = NSA_TQ
    assert t % tq == 0 and tq >= REL_MAX_DIST and NSA_WINDOW % tq == 0 and tq % SEL_BLOCK == 0
    assert NSA_NEAR < LANE
    n_cmp = kc.shape[2]
    n_blk = t // SEL_BLOCK
    n_sel = min(SEL_COUNT, n_blk)
    kvspec = pl.BlockSpec((1, 1, t, dh), lambda i, j, k: (i, j, 0, 0))
    cspec = pl.BlockSpec((1, 1, n_cmp, dh), lambda i, j, k: (i, j, 0, 0))
    return pl.pallas_call(
        functools.partial(_nsa_attn_kernel, n_sel=n_sel),
        grid=(b, g, t // tq),
        in_specs=[pl.BlockSpec(memory_space=pltpu.SMEM),
                  pl.BlockSpec((1, nq, tq, dh), lambda i, j, k: (i, j, k, 0)),
                  cspec, cspec, kvspec, kvspec, kvspec, kvspec,
                  pl.BlockSpec((1, 1, 16, tq), lambda i, j, k: (i, j, 0, k))],
        out_specs=pl.BlockSpec((1, tq, nq * dh), lambda i, j, k: (i, k, j)),
        out_shape=jax.ShapeDtypeStruct((b, t, g * nq * dh), BF16),
        scratch_shapes=[pltpu.VMEM((nq, tq, tq), F32),
                        pltpu.VMEM((nq, tq, tq), F32),
                        pltpu.VMEM((3, nq, tq, LANE), BF16),
                        pltpu.VMEM((n_blk, tq), F32),
                        pltpu.VMEM((tq, n_blk), BF16),
                        pltpu.VMEM((nq, tq, 1), F32),
                        pltpu.VMEM((nq, tq, 1), F32),
                        pltpu.VMEM((nq, tq, dh), F32)],
        compiler_params=_cparams(("parallel", "parallel", "arbitrary")),
        name="nsa_attn",
    )(tab, q, kc, vc, ks, vs, kw, vw, gate)


def _nsa_layer(x, w_in, w_out, cmp_pos, cmp_w1, cmp_w2, rel_bias, gain, bias, alpha):
    b, t, d = x.shape
    g, nq, dh = NSA_KV_HEADS, NSA_QPG, HEAD_DIM
    kvw = g * dh
    n_in = w_in.shape[1]
    n_pad = -n_in % (3 * LANE)
    w_in_p = jnp.pad(w_in, ((0, 0), (0, n_pad))).astype(BF16)
    x2 = x.reshape(b * t, d)
    h = _proj(x2, w_in_p, tn=3 * LANE).reshape(b, t, n_in + n_pad)

    def heads(lo, n_heads, dt):
        z = h[:, :, lo:lo + n_heads * dh].reshape(b, t, n_heads, dh)
        return z.transpose(0, 2, 1, 3).astype(dt)

    q = heads(0, ATTN_HEADS, BF16)
    off = ATTN_HEADS * dh
    zk = heads(off, g, F32).reshape(b, g, t // CMP_STRIDE, CMP_STRIDE * dh)
    zv = heads(off + kvw, g, F32).reshape(b, g, t // CMP_STRIDE, CMP_STRIDE * dh)
    ks = heads(off + 2 * kvw, g, BF16)
    vs = heads(off + 3 * kvw, g, BF16)
    kw = heads(off + 4 * kvw, g, BF16)
    vw = heads(off + 5 * kvw, g, BF16)
    gate = h[:, :, off + 6 * kvw:off + 6 * kvw + 3 * ATTN_HEADS].reshape(b, t, g, 3 * nq)
    gate = jnp.pad(gate.transpose(0, 2, 3, 1), ((0, 0), (0, 0), (0, 16 - 3 * nq), (0, 0)))

    pos = cmp_pos.reshape(2, 2, 1, CMP_STRIDE * dh)
    w1 = cmp_w1.reshape(2, 2, CMP_STRIDE * dh, cmp_w1.shape[-1]).astype(BF16)
    kc, vc = _nsa_compress(zk, zv, pos, w1, cmp_w2.astype(BF16))
    o = _nsa_attention(rel_bias, q, kc, vc, ks, vs, kw, vw, gate)
    y = _proj_ln(o.reshape(b * t, d), w_out.astype(BF16), x2, gain, bias, alpha)
    return y.reshape(b, t, d)


def _hgrn_kernel(lbp_ref, gain_ref, zq_ref, zf_ref, zi_ref, zg_ref, o_ref, st_ref, *, layer):
    C = HGRN_CHUNK
    SB = HGRN_SUB
    t = zq_ref.shape[1]

    p = lbp_ref[...]
    e = jnp.exp(p - jnp.max(p, axis=0, keepdims=True))
    sm = e / jnp.sum(e, axis=0, keepdims=True)
    cs = sm[0:1]
    for r in range(1, layer + 1):
        cs = cs + sm[r:r + 1]
    lb = cs - sm[0:1]
    log_lb = jnp.log(lb)
    log_1m = jnp.log1p(-lb)
    gain = gain_ref[...]

    st_ref[...] = jnp.zeros(st_ref.shape, F32)
    tri = jnp.where(lax.broadcasted_iota(jnp.int32, (C, C), 0) >= lax.broadcasted_iota(jnp.int32, (C, C), 1),
                    1.0, 0.0).astype(BF16)
    srow = lax.broadcasted_iota(jnp.int32, (C, 1), 0)
    trow = lax.broadcasted_iota(jnp.int32, (SB, 1), 0)
    lane = lax.broadcasted_iota(jnp.int32, (SB, C), 1)

    def chunk(c, carry):
        r0 = pl.multiple_of(c * C, C)
        zq = zq_ref[0, pl.ds(r0, C), :]
        zf = zf_ref[0, pl.ds(r0, C), :]
        v = zi_ref[0, pl.ds(r0, C), :]
        zg = zg_ref[0, pl.ds(r0, C), :]
        q = jax.nn.silu(zq)
        log_f = jnp.logaddexp(log_lb, log_1m + jax.nn.log_sigmoid(zf))
        kk = (1.0 - lb) * jax.nn.sigmoid(-zf)
        vb = v.astype(BF16)

        g_hi, g_mid, g_lo = _split3(log_f)
        bcum = _dot(tri, g_hi) + _dot(tri, g_mid) + _dot(tri, g_lo)
        b_last = bcum[C - 1:C, :]

        st = st_ref[...]
        o = _dot_nt((q * jnp.exp(bcum)).astype(BF16), st.astype(BF16))

        parts = []
        for sb in range(C // SB):
            lo = sb * SB
            b_i = bcum[lo:lo + SB]
            q_i = q[lo:lo + SB]
            if sb > 0:
                ref = bcum[lo - 1:lo, :]
                q_side = q_i * jnp.exp(b_i - ref)
                k_side = jnp.where(srow < lo, kk * jnp.exp(jnp.minimum(ref - bcum, 0.0)), 0.0)
                a = _dot_nt(q_side.astype(BF16), k_side.astype(BF16))
            else:
                a = jnp.zeros((SB, C), F32)
            for s in range(SB):
                w = jnp.where(trow >= s, jnp.exp(jnp.minimum(b_i - b_i[s:s + 1], 0.0)), 0.0)
                col = jnp.sum(q_i * kk[lo + s:lo + s + 1] * w, axis=-1, keepdims=True)
                a = jnp.where(lane == lo + s, col, a)
            parts.append(_dot(a.astype(BF16), vb))
        o = o + jnp.concatenate(parts, axis=0)

        kd = kk * jnp.exp(b_last - bcum)
        st_ref[...] = st * jnp.exp(b_last) + _dot(v.T.astype(BF16), kd.astype(BF16))

        o = o * lax.rsqrt(jnp.mean(o * o, axis=-1, keepdims=True) + RMS_EPS) * gain
        o = o * jax.nn.silu(zg)
        o_ref[0, pl.ds(r0, C), :] = o.astype(o_ref.dtype)
        return carry

    lax.fori_loop(0, t // C, chunk, 0)


def _hgrn_layer(x, w_in, w_out, norm_gain, lb_param, layer, gain, bias, alpha):
    b, t, d = x.shape
    nh = HGRN_HEADS
    kd = d // nh
    x2 = x.reshape(b * t, d)
    h = _proj(x2, w_in.astype(BF16), tn=1024).reshape(b, t, 4 * d)

    def zspec(part):
        return pl.BlockSpec((1, t, kd), lambda i, j: (i, 0, part * nh + j))

    o = pl.pallas_call(
        functools.partial(_hgrn_kernel, layer=layer),
        grid=(b, nh),
        in_specs=[pl.BlockSpec((lb_param.shape[0], kd), lambda i, j: (0, j)),
                  pl.BlockSpec((1, kd), lambda i, j: (0, 0)),
                  zspec(0), zspec(1), zspec(2), zspec(3)],
        out_specs=pl.BlockSpec((1, t, kd), lambda i, j: (i, 0, j)),
        out_shape=jax.ShapeDtypeStruct((b, t, d), BF16),
        scratch_shapes=[pltpu.VMEM((kd, kd), F32)],
        compiler_params=_cparams(("parallel", "parallel")),
        name="hgrn",
    )(lb_param, norm_gain.reshape(1, kd), h, h, h, h)
    y = _proj_ln(o.reshape(b * t, d), w_out.astype(BF16), x2, gain, bias, alpha)
    return y.reshape(b, t, d)


def _swa_kernel(tab_ref, sink_ref, q_ref, k_ref, v_ref, o_ref, bd_ref, bp_ref):
    L = SWA_WINDOW
    nq = SWA_QPG
    dh = HEAD_DIM
    g = pl.program_id(1)
    n = pl.program_id(2)
    ii = lax.broadcasted_iota(jnp.int32, (L, L), 0)
    jj = lax.broadcasted_iota(jnp.int32, (L, L), 1)

    @pl.when(n == 0)
    def _init():
        b_diag = _t5_bucket(ii - jj)
        b_prev = _t5_bucket(ii - jj + L)
        for h in range(nq):
            bd_ref[h] = _bias_lookup(b_diag, tab_ref, g * nq + h)
            bp_ref[h] = _bias_lookup(b_prev, tab_ref, g * nq + h)

    qs = (q_ref[0].astype(F32) * (dh ** -0.5)).astype(BF16).reshape(nq * L, dh)
    cur = pl.multiple_of(n * L, L)
    prev = pl.multiple_of(jnp.maximum(n - 1, 0) * L, L)
    k_c = k_ref[0, 0, pl.ds(cur, L), :]
    v_c = v_ref[0, 0, pl.ds(cur, L), :]
    k_p = k_ref[0, 0, pl.ds(prev, L), :]
    v_p = v_ref[0, 0, pl.ds(prev, L), :]

    s_c = _dot_nt(qs, k_c).reshape(nq, L, L) + bd_ref[...]
    s_p = _dot_nt(qs, k_p).reshape(nq, L, L) + bp_ref[...]
    s_c = jnp.where((ii >= jj)[None], s_c, NEG_INF)
    s_p = jnp.where(((jj > ii) & (n > 0))[None], s_p, NEG_INF)
    sink = _per_head_scalar(nq, lambda h: sink_ref[g * nq + h])
    m = jnp.maximum(jnp.maximum(jnp.max(s_c, axis=-1, keepdims=True), jnp.max(s_p, axis=-1, keepdims=True)), sink)
    e_c = jnp.exp(s_c - m)
    e_p = jnp.exp(s_p - m)
    den = jnp.sum(e_c, axis=-1, keepdims=True) + jnp.sum(e_p, axis=-1, keepdims=True) + jnp.exp(sink - m)
    pv = (_dot(e_c.reshape(nq * L, L).astype(BF16), v_c) + _dot(e_p.reshape(nq * L, L).astype(BF16), v_p))
    o = pv.reshape(nq, L, dh) / den
    o_ref[0] = jnp.concatenate([o[h] for h in range(nq)], axis=1).astype(o_ref.dtype)


def _swa_layer(x, w_in, w_out, sinks, rel_bias, gain, bias, alpha):
    b, t, d = x.shape
    kvh, nq, dh, L = SWA_KV_HEADS, SWA_QPG, HEAD_DIM, SWA_WINDOW
    assert t % L == 0 and L >= REL_MAX_DIST
    x2 = x.reshape(b * t, d)
    n_in = w_in.shape[1]
    h = _proj(x2, w_in.astype(BF16), tn=n_in // 2).reshape(b, t, n_in)

    def heads(lo, n_heads):
        z = h[:, :, lo:lo + n_heads * dh].reshape(b, t, n_heads, dh)
        return z.transpose(0, 2, 1, 3).astype(BF16)

    q = heads(0, ATTN_HEADS)
    k = heads(ATTN_HEADS * dh, kvh)
    v = heads(ATTN_HEADS * dh + kvh * dh, kvh)
    kvspec = pl.BlockSpec((1, 1, t, dh), lambda i, j, n: (i, j, 0, 0))
    o = pl.pallas_call(
        _swa_kernel,
        grid=(b, kvh, t // L),
        in_specs=[pl.BlockSpec(memory_space=pltpu.SMEM),
                  pl.BlockSpec(memory_space=pltpu.SMEM),
                  pl.BlockSpec((1, nq, L, dh), lambda i, j, n: (i, j, n, 0)),
                  kvspec, kvspec],
        out_specs=pl.BlockSpec((1, L, nq * dh), lambda i, j, n: (i, n, j)),
        out_shape=jax.ShapeDtypeStruct((b, t, d), BF16),
        scratch_shapes=[pltpu.VMEM((nq, L, L), F32), pltpu.VMEM((nq, L, L), F32)],
        compiler_params=_cparams(("parallel", "parallel", "arbitrary")),
        name="swa_attn",
    )(rel_bias, sinks, q, k, v)
    y = _proj_ln(o.reshape(b * t, d), w_out.astype(BF16), x2, gain, bias, alpha)
    return y.reshape(b, t, d)


def kernel(x, rel_bias, ln_gain, ln_bias, ffn1_w_gate, ffn1_w_up, ffn1_w_down, ffn2_w_gate, ffn2_w_up,
           ffn2_w_down, nsa_w_in, nsa_w_out, nsa_cmp_pos, nsa_cmp_w1, nsa_cmp_w2, hgrn_w_in, hgrn_w_out,
           hgrn_norm_gain, hgrn_lb, swa_w_in, swa_w_out, swa_sinks):
    depth = ln_gain.shape[0]
    alpha = (2.0 * depth) ** 0.25
    b, t, d = x.shape

    def ffn(x, wg, wu, wd, gain, bias):
        y = _ffn_ln(x.reshape(b * t, d), wg.astype(BF16), wu.astype(BF16), wd.astype(BF16), gain, bias, alpha)
        return y.reshape(b, t, d)

    for i in range(depth):
        x = ffn(x, ffn1_w_gate[i], ffn1_w_up[i], ffn1_w_down[i], ln_gain[i, 0], ln_bias[i, 0])
        kind, slot = i % N_MIXERS, i // N_MIXERS
        if kind == 0:
            x = _nsa_layer(x, nsa_w_in[slot], nsa_w_out[slot], nsa_cmp_pos[slot], nsa_cmp_w1[slot],
                           nsa_cmp_w2[slot], rel_bias, ln_gain[i, 1], ln_bias[i, 1], alpha)
        elif kind == 1:
            x = _hgrn_layer(x, hgrn_w_in[slot], hgrn_w_out[slot], hgrn_norm_gain[slot], hgrn_lb, i,
                            ln_gain[i, 1], ln_bias[i, 1], alpha)
        else:
            x = _swa_layer(x, swa_w_in[slot], swa_w_out[slot], swa_sinks[slot], rel_bias,
                           ln_gain[i, 1], ln_bias[i, 1], alpha)
        x = ffn(x, ffn2_w_gate[i], ffn2_w_up[i], ffn2_w_down[i], ln_gain[i, 2], ln_bias[i, 2])
    return x
```

```python
import functools
import math

import jax
import jax.numpy as jnp
from jax import lax
from jax.experimental import pallas as pl
from jax.experimental.pallas import tpu as pltpu

F32 = jnp.float32
BF16 = jnp.bfloat16

DEPTH = 4
N_MIXERS = 3
REL_BUCKETS = 32
REL_EXACT = REL_BUCKETS // 2
REL_MAX_DIST = 128
ATTN_HEADS = 16
HEAD_DIM = 64
NSA_KV_HEADS = 4
NSA_QPG = ATTN_HEADS // NSA_KV_HEADS
CMP_STRIDE = 16
CMP_LEN = 2 * CMP_STRIDE
SEL_BLOCK = 64
SEL_COUNT = 16
NSA_WINDOW = 512
HGRN_HEADS = 8
HGRN_CHUNK = 64
HGRN_SUB = 16
SWA_KV_HEADS = 2
SWA_QPG = ATTN_HEADS // SWA_KV_HEADS
SWA_WINDOW = 128
LN_EPS = 1e-5
RMS_EPS = 1e-6
NEG_INF = -1e30

LANE = 128
VMEM_LIMIT = 48 * 1024 * 1024
NSA_TQ = 256
NSA_TKF = 512
NSA_NEAR = NSA_TQ // CMP_STRIDE + 8


def _cparams(sem):
    return pltpu.CompilerParams(dimension_semantics=sem, vmem_limit_bytes=VMEM_LIMIT)


def _dot(a, b):
    return jnp.dot(a, b, preferred_element_type=F32)


def _dot_nt(a, b):
    return lax.dot_general(a, b, (((1,), (1,)), ((), ())), preferred_element_type=F32)


def _layer_norm_rows(y, g, b):
    mu = jnp.mean(y, axis=-1, keepdims=True)
    yc = y - mu
    var = jnp.mean(yc * yc, axis=-1, keepdims=True)
    return yc * lax.rsqrt(var + LN_EPS) * g + b


def _ffn_ln_kernel(x_ref, wg_ref, wu_ref, wd_ref, g_ref, b_ref, o_ref, xb_ref, acc_ref, *, alpha):
    j = pl.program_id(1)

    @pl.when(j == 0)
    def _():
        xb_ref[...] = x_ref[...].astype(BF16)
        acc_ref[...] = jnp.zeros_like(acc_ref)

    xb = xb_ref[...]
    gate = _dot(xb, wg_ref[...])
    up = _dot(xb, wu_ref[...])
    h = (jax.nn.silu(gate) * up).astype(BF16)
    acc_ref[...] += _dot(h, wd_ref[...])

    @pl.when(j == pl.num_programs(1) - 1)
    def _():
        y = alpha * x_ref[...] + 0.5 * acc_ref[...]
        o_ref[...] = _layer_norm_rows(y, g_ref[...], b_ref[...])


def _ffn_ln(x, wg, wu, wd, gain, bias, alpha, tm=512, tf=1408):
    n, d = x.shape
    f = wg.shape[1]
    tm = min(tm, n)
    assert n % tm == 0 and f % tf == 0
    return pl.pallas_call(
        functools.partial(_ffn_ln_kernel, alpha=alpha),
        grid=(n // tm, f // tf),
        in_specs=[
            pl.BlockSpec((tm, d), lambda i, j: (i, 0)),
            pl.BlockSpec((d, tf), lambda i, j: (0, j)),
            pl.BlockSpec((d, tf), lambda i, j: (0, j)),
            pl.BlockSpec((tf, d), lambda i, j: (j, 0)),
            pl.BlockSpec((1, d), lambda i, j: (0, 0)),
            pl.BlockSpec((1, d), lambda i, j: (0, 0)),
        ],
        out_specs=pl.BlockSpec((tm, d), lambda i, j: (i, 0)),
        out_shape=jax.ShapeDtypeStruct((n, d), F32),
        scratch_shapes=[pltpu.VMEM((tm, d), BF16), pltpu.VMEM((tm, d), F32)],
        compiler_params=_cparams(("parallel", "arbitrary")),
        name="ffn_ln",
    )(x, wg, wu, wd, gain.reshape(1, d), bias.reshape(1, d))


def _proj_kernel(x_ref, w_ref, o_ref):
    o_ref[...] = _dot(x_ref[...].astype(BF16), w_ref[...]).astype(o_ref.dtype)


def _proj(x, w, tn, tm=1024, out_dtype=F32):
    n, k = x.shape
    m = w.shape[1]
    tm = min(tm, n)
    assert n % tm == 0 and m % tn == 0
    return pl.pallas_call(
        _proj_kernel,
        grid=(n // tm, m // tn),
        in_specs=[pl.BlockSpec((tm, k), lambda i, j: (i, 0)),
                  pl.BlockSpec((k, tn), lambda i, j: (0, j))],
        out_specs=pl.BlockSpec((tm, tn), lambda i, j: (i, j)),
        out_shape=jax.ShapeDtypeStruct((n, m), out_dtype),
        compiler_params=_cparams(("parallel", "arbitrary")),
        name="proj",
    )(x, w)


def _proj_ln_kernel(a_ref, w_ref, r_ref, g_ref, b_ref, o_ref, *, alpha):
    y = _dot(a_ref[...].astype(BF16), w_ref[...])
    o_ref[...] = _layer_norm_rows(alpha * r_ref[...] + y, g_ref[...], b_ref[...])


def _proj_ln(a, w, res, gain, bias, alpha, tm=512):
    n, k = a.shape
    d = w.shape[1]
    tm = min(tm, n)
    assert n % tm == 0
    return pl.pallas_call(
        functools.partial(_proj_ln_kernel, alpha=alpha),
        grid=(n // tm,),
        in_specs=[pl.BlockSpec((tm, k), lambda i: (i, 0)),
                  pl.BlockSpec((k, d), lambda i: (0, 0)),
                  pl.BlockSpec((tm, d), lambda i: (i, 0)),
                  pl.BlockSpec((1, d), lambda i: (0, 0)),
                  pl.BlockSpec((1, d), lambda i: (0, 0))],
        out_specs=pl.BlockSpec((tm, d), lambda i: (i, 0)),
        out_shape=jax.ShapeDtypeStruct((n, d), F32),
        compiler_params=_cparams(("parallel",)),
        name="proj_ln",
    )(a, w, res, gain.reshape(1, d), bias.reshape(1, d))


def _t5_bucket(dist):
    n = jnp.maximum(dist, 0)
    nf = jnp.maximum(n, 1).astype(F32)
    large = REL_EXACT + (jnp.log(nf / REL_EXACT) / math.log(REL_MAX_DIST / REL_EXACT)
                         * (REL_BUCKETS - REL_EXACT)).astype(jnp.int32)
    return jnp.where(n < REL_EXACT, n, jnp.minimum(large, REL_BUCKETS - 1))


def _bias_lookup(bucket, tab_ref, col):
    out = jnp.zeros(bucket.shape, F32)
    for k in range(REL_BUCKETS):
        out = jnp.where(bucket == k, tab_ref[k, col], out)
    return out


def _per_head_scalar(n_heads, fn):
    hidx = lax.broadcasted_iota(jnp.int32, (n_heads, 1, 1), 0)
    out = jnp.zeros((n_heads, 1, 1), F32)
    for h in range(n_heads):
        out = jnp.where(hidx == h, fn(h), out)
    return out


def _split3(x):
    hi = x.astype(BF16)
    r1 = x - hi.astype(F32)
    mid = r1.astype(BF16)
    lo = (r1 - mid.astype(F32)).astype(BF16)
    return hi, mid, lo


def _nsa_compress_kernel(zk_ref, zv_ref, pos_ref, w1_ref, w2_ref, w2t_ref, kc_ref, vct_ref):
    def hidden(idx, z_ref):
        z = z_ref[0, 0]
        u = _dot((z + pos_ref[idx, 0]).astype(BF16), w1_ref[idx, 0])
        v = _dot((z + pos_ref[idx, 1]).astype(BF16), w1_ref[idx, 1])
        return jax.nn.gelu(u + pltpu.roll(v, z.shape[0] - 1, 0)).astype(BF16)

    kc_ref[0, 0] = _dot(hidden(0, zk_ref), w2_ref[0]).astype(kc_ref.dtype)
    vct_ref[0, 0] = _dot_nt(w2t_ref[1], hidden(1, zv_ref)).astype(vct_ref.dtype)


def _nsa_compress(zk, zv, pos, w1, w2):
    b, g, nc, kd = zk.shape
    hid = w1.shape[-1]
    dh = w2.shape[-1]
    zspec = pl.BlockSpec((1, 1, nc, kd), lambda i, j: (i, j, 0, 0))
    return pl.pallas_call(
        _nsa_compress_kernel,
        grid=(b, g),
        in_specs=[zspec, zspec,
                  pl.BlockSpec((2, 2, 1, kd), lambda i, j: (0, 0, 0, 0)),
                  pl.BlockSpec((2, 2, kd, hid), lambda i, j: (0, 0, 0, 0)),
                  pl.BlockSpec((2, hid, dh), lambda i, j: (0, 0, 0)),
                  pl.BlockSpec((2, dh, hid), lambda i, j: (0, 0, 0))],
        out_specs=[pl.BlockSpec((1, 1, nc, dh), lambda i, j: (i, j, 0, 0)),
                   pl.BlockSpec((1, 1, dh, nc), lambda i, j: (i, j, 0, 0))],
        out_shape=[jax.ShapeDtypeStruct((b, g, nc, dh), BF16), jax.ShapeDtypeStruct((b, g, dh, nc), BF16)],
        compiler_params=_cparams(("parallel", "parallel")),
        name="nsa_compress",
    )(zk, zv, pos, w1, w2, w2.transpose(0, 2, 1))


def _nsa_attn_kernel(tab_ref, qt_ref, kc_ref, vct_ref, ks_ref, vst_ref, kw_ref, vwt_ref, gate_ref, o_ref,
                     bn_ref, edge_ref, cb_ref, vt_ref, neg_ref, add_ref, m_ref, l_ref, acc_ref, *, n_sel):
    tq = NSA_TQ
    tkf = NSA_TKF
    nq = NSA_QPG
    dh = HEAD_DIM
    w = nq * tq
    g = pl.program_id(1)
    iq = pl.program_id(2)
    t0 = iq * tq
    n_cmp = kc_ref.shape[2]
    n_blk = vt_ref.shape[0]
    bpt = tq // SEL_BLOCK
    far_bucket = REL_BUCKETS - 1

    def head_cols(fn):
        return jnp.concatenate([fn(h) for h in range(nq)], axis=1)

    far_row = head_cols(lambda h: jnp.full((1, tq), tab_ref[far_bucket, g * nq + h], F32))

    @pl.when(iq == 0)
    def _init():
        jj = lax.broadcasted_iota(jnp.int32, (tq, tq), 0)
        ii = lax.broadcasted_iota(jnp.int32, (tq, tq), 1)
        d = ii - jj
        b_diag = _t5_bucket(d)
        b_prev = _t5_bucket(d + tq)
        mm = lax.broadcasted_iota(jnp.int32, (LANE, tq), 0)
        i2 = lax.broadcasted_iota(jnp.int32, (LANE, tq), 1)
        dn = i2 - CMP_STRIDE * mm + (9 * CMP_STRIDE - CMP_LEN + 1)
        b_near = _t5_bucket(dn)
        for h in range(nq):
            col = g * nq + h
            far = tab_ref[far_bucket, col]
            cols = slice(h * tq, (h + 1) * tq)
            bn_ref[0:tq, cols] = _bias_lookup(b_prev, tab_ref, col)
            bn_ref[tq:2 * tq, cols] = jnp.where(d >= 0, _bias_lookup(b_diag, tab_ref, col), NEG_INF)
            edge_ref[:, cols] = jnp.where(jj > ii, far, NEG_INF)
            near = jnp.where(dn >= 0, _bias_lookup(b_near, tab_ref, col), NEG_INF)
            cbv = jnp.where(mm < NSA_NEAR, near,
                            jnp.where(mm == NSA_NEAR, far, jnp.where(mm == NSA_NEAR + 1, NEG_INF, 0.0)))
            hi, mid, lo = _split3(cbv)
            cb_ref[0, :, cols] = hi
            cb_ref[1, :, cols] = mid
            cb_ref[2, :, cols] = lo

    qt = (head_cols(lambda h: qt_ref[0, h]).astype(F32) * (dh ** -0.5)).astype(BF16)

    n_lo = t0 // CMP_STRIDE - 9
    ni = lax.broadcasted_iota(jnp.int32, (n_cmp, LANE), 0)
    mi = lax.broadcasted_iota(jnp.int32, (n_cmp, LANE), 1)
    place = (((mi < NSA_NEAR) & (ni == mi + n_lo)) | ((mi == NSA_NEAR) & (ni < n_lo))
             | ((mi == NSA_NEAR + 1) & (ni >= n_lo + NSA_NEAR)))
    shift = jnp.where(place, 1.0, 0.0).astype(BF16)
    lcm = (_dot(kc_ref[0, 0], qt) + _dot(shift, cb_ref[0]) + _dot(shift, cb_ref[1]) + _dot(shift, cb_ref[2]))
    ec = jnp.exp(lcm - jnp.max(lcm, axis=0, keepdims=True))
    sc = jnp.sum(ec, axis=0, keepdims=True)
    tl = t0 + lax.broadcasted_iota(jnp.int32, (1, w), 1) % tq
    pct = ec * jnp.where(tl >= CMP_LEN - 1, 1.0 / sc, 0.0)
    o_ct = _dot(vct_ref[0, 0], pct.astype(BF16))

    psum = pct[:, 0:tq]
    for h in range(1, nq):
        psum = psum + pct[:, h * tq:(h + 1) * tq]
    p_hi = psum.astype(BF16)
    p_lo = (psum - p_hi.astype(F32)).astype(BF16)
    jb = lax.broadcasted_iota(jnp.int32, (n_blk, n_cmp), 0)
    nb = lax.broadcasted_iota(jnp.int32, (n_blk, n_cmp), 1)
    ovl = jnp.where((CMP_STRIDE * nb < SEL_BLOCK * (jb + 1)) & (CMP_STRIDE * nb + CMP_LEN > SEL_BLOCK * jb),
                    1.0, 0.0).astype(BF16)
    imp_t = _dot(ovl, p_hi) + _dot(ovl, p_lo)
    jb2 = lax.broadcasted_iota(jnp.int32, (n_blk, tq), 0)
    cur = (t0 + lax.broadcasted_iota(jnp.int32, (n_blk, tq), 1)) // SEL_BLOCK
    forced = (jb2 == 0) | (jb2 == cur) | (jb2 == cur - 1)
    val = jnp.where(jb2 > cur, -1e9, jnp.where(forced, 1e9, imp_t))
    vt_ref[...] = val

    def rank_body(jp, rank):
        row = vt_ref[pl.ds(jp, 1), :]
        beats = (row > val) | ((row == val) & (jp < jb2))
        return rank + jnp.where(beats, 1, 0)

    n_causal = (t0 + tq) // SEL_BLOCK
    rank = lax.fori_loop(0, jnp.where(n_causal <= n_sel, 0, n_causal), rank_body,
                         jnp.zeros((n_blk, tq), jnp.int32))
    neg_t = jnp.where(rank < n_sel, 0.0, NEG_INF)
    neg4 = jnp.concatenate([neg_t] * nq, axis=1)
    neg_ref[...] = neg4
    add_ref[...] = neg4 + far_row

    def kv_tile(k_ref, v_ref, key0, n_keys):
        start = pl.multiple_of(key0 + tq, LANE)
        return k_ref[0, 0, pl.ds(start, n_keys), :], v_ref[0, 0, :, pl.ds(start, n_keys)]

    def add_block_rows(s, rows):
        return jnp.concatenate([s[r * SEL_BLOCK:(r + 1) * SEL_BLOCK] + row for r, row in enumerate(rows)], axis=0)

    def flash_first(s, vt):
        m = jnp.max(s, axis=0, keepdims=True)
        p = jnp.exp(s - m)
        m_ref[...] = m
        l_ref[...] = jnp.sum(p, axis=0, keepdims=True)
        acc_ref[...] = _dot(vt, p.astype(BF16))

    def flash_next(s, vt):
        m_old = m_ref[...]
        m_new = jnp.maximum(m_old, jnp.max(s, axis=0, keepdims=True))
        p = jnp.exp(s - m_new)
        a = jnp.exp(m_old - m_new)
        l_ref[...] = a * l_ref[...] + jnp.sum(p, axis=0, keepdims=True)
        acc_ref[...] = a * acc_ref[...] + _dot(vt, p.astype(BF16))
        m_ref[...] = m_new

    pad_row = jnp.where(iq > 0, jnp.zeros((1, w), F32), NEG_INF)
    blk0 = iq * bpt

    k, vt = kv_tile(ks_ref, vst_ref, t0 - tq, 2 * tq)
    s = _dot(k, qt) + bn_ref[...]
    near_rows = [neg_ref[pl.ds(jnp.maximum(blk0 - bpt + r, 0), 1), :] + pad_row for r in range(bpt)]
    near_rows += [neg_ref[pl.ds(blk0 + r, 1), :] for r in range(bpt)]
    flash_first(add_block_rows(s, near_rows), vt)

    n_far = jnp.maximum(blk0 - bpt, 0)
    bpf = tkf // SEL_BLOCK

    def far_body(c, carry):
        k, vt = kv_tile(ks_ref, vst_ref, c * tkf, tkf)
        rows = [jnp.where(c * bpf + r < n_far, add_ref[pl.ds(c * bpf + r, 1), :], NEG_INF) for r in range(bpf)]
        flash_next(add_block_rows(_dot(k, qt), rows), vt)
        return carry

    lax.fori_loop(0, (n_far + bpf - 1) // bpf, far_body, 0)
    o_st = acc_ref[...] * (1.0 / l_ref[...])

    k, vt = kv_tile(kw_ref, vwt_ref, t0 - tq, 2 * tq)
    s = _dot(k, qt) + bn_ref[...]
    flash_first(jnp.concatenate([s[0:tq] + pad_row, s[tq:2 * tq]], axis=0), vt)

    @pl.when(iq >= 2)
    def _():
        k, vt = kv_tile(kw_ref, vwt_ref, t0 - 2 * tq, tq)
        flash_next(_dot(k, qt) + edge_ref[...], vt)

    o_wt = acc_ref[...] * (1.0 / l_ref[...])

    sg = jax.nn.sigmoid(gate_ref[0, 0])
    outs = []
    for h in range(nq):
        cols = slice(h * tq, (h + 1) * tq)
        outs.append(sg[3 * h:3 * h + 1] * o_ct[:, cols] + sg[3 * h + 1:3 * h + 2] * o_st[:, cols]
                    + sg[3 * h + 2:3 * h + 3] * o_wt[:, cols])
    o_ref[0] = jnp.concatenate(outs, axis=0).T.astype(o_ref.dtype)


def _nsa_attention(tab, qt, kc, vct, ks, vst, kw, vwt, gate):
    b, _, dh, t = qt.shape
    g = NSA_KV_HEADS
    nq = NSA_QPG
    tq = NSA_TQ
    w = nq * tq
    assert t % tq == 0 and tq >= REL_MAX_DIST and NSA_WINDOW == 2 * tq and tq % SEL_BLOCK == 0
    assert NSA_TKF % SEL_BLOCK == 0 and 2 * tq >= NSA_TKF and t >= NSA_TKF and tq % LANE == 0
    assert NSA_NEAR + 1 < LANE
    n_cmp = kc.shape[2]
    n_blk = t // SEL_BLOCK
    n_sel = min(SEL_COUNT, n_blk)
    assert n_sel >= 3
    tp = ks.shape[2]
    assert tp == t + tq and vst.shape[3] == tp
    kspec = pl.BlockSpec((1, 1, tp, dh), lambda i, j, k: (i, j, 0, 0))
    vtspec = pl.BlockSpec((1, 1, dh, tp), lambda i, j, k: (i, j, 0, 0))
    return pl.pallas_call(
        functools.partial(_nsa_attn_kernel, n_sel=n_sel),
        grid=(b, g, t // tq),
        in_specs=[pl.BlockSpec(memory_space=pltpu.SMEM),
                  pl.BlockSpec((1, nq, dh, tq), lambda i, j, k: (i, j, 0, k)),
                  pl.BlockSpec((1, 1, n_cmp, dh), lambda i, j, k: (i, j, 0, 0)),
                  pl.BlockSpec((1, 1, dh, n_cmp), lambda i, j, k: (i, j, 0, 0)),
                  kspec, vtspec, kspec, vtspec,
                  pl.BlockSpec((1, 1, 16, tq), lambda i, j, k: (i, j, 0, k))],
        out_specs=pl.BlockSpec((1, tq, nq * dh), lambda i, j, k: (i, k, j)),
        out_shape=jax.ShapeDtypeStruct((b, t, g * nq * dh), BF16),
        scratch_shapes=[pltpu.VMEM((2 * tq, w), F32),
                        pltpu.VMEM((tq, w), F32),
                        pltpu.VMEM((3, LANE, w), BF16),
                        pltpu.VMEM((n_blk, tq), F32),
                        pltpu.VMEM((n_blk, w), F32),
                        pltpu.VMEM((n_blk, w), F32),
                        pltpu.VMEM((1, w), F32),
                        pltpu.VMEM((1, w), F32),
                        pltpu.VMEM((dh, w), F32)],
        compiler_params=_cparams(("parallel", "parallel", "arbitrary")),
        name="nsa_attn",
    )(tab, qt, kc, vct, ks, vst, kw, vwt, gate)


def _nsa_layer(x, w_in, w_out, cmp_pos, cmp_w1, cmp_w2, rel_bias, gain, bias, alpha):
    b, t, d = x.shape
    g, nq, dh = NSA_KV_HEADS, NSA_QPG, HEAD_DIM
    kvw = g * dh
    n_in = w_in.shape[1]
    n_pad = -n_in % (3 * LANE)
    w_in_p = jnp.pad(w_in, ((0, 0), (0, n_pad))).astype(BF16)
    x2 = x.reshape(b * t, d)
    h = _proj(x2, w_in_p, tn=3 * LANE).reshape(b, t, n_in + n_pad)

    def heads(lo, n_heads, dt, perm=(0, 2, 1, 3)):
        z = h[:, :, lo:lo + n_heads * dh].reshape(b, t, n_heads, dh)
        return z.transpose(perm).astype(dt)

    head_t = (0, 2, 3, 1)
    qt = heads(0, ATTN_HEADS, BF16, head_t)
    off = ATTN_HEADS * dh
    zk = heads(off, g, F32).reshape(b, g, t // CMP_STRIDE, CMP_STRIDE * dh)
    zv = heads(off + kvw, g, F32).reshape(b, g, t // CMP_STRIDE, CMP_STRIDE * dh)
    pad_k = ((0, 0), (0, 0), (NSA_TQ, 0), (0, 0))
    pad_vt = ((0, 0), (0, 0), (0, 0), (NSA_TQ, 0))
    ks = jnp.pad(heads(off + 2 * kvw, g, BF16), pad_k)
    vst = jnp.pad(heads(off + 3 * kvw, g, BF16, head_t), pad_vt)
    kw = jnp.pad(heads(off + 4 * kvw, g, BF16), pad_k)
    vwt = jnp.pad(heads(off + 5 * kvw, g, BF16, head_t), pad_vt)
    gate = h[:, :, off + 6 * kvw:off + 6 * kvw + 3 * ATTN_HEADS].reshape(b, t, g, 3 * nq)
    gate = jnp.pad(gate.transpose(0, 2, 3, 1), ((0, 0), (0, 0), (0, 16 - 3 * nq), (0, 0)))

    pos = cmp_pos.reshape(2, 2, 1, CMP_STRIDE * dh)
    w1 = cmp_w1.reshape(2, 2, CMP_STRIDE * dh, cmp_w1.shape[-1]).astype(BF16)
    kc, vct = _nsa_compress(zk, zv, pos, w1, cmp_w2.astype(BF16))
    o = _nsa_attention(rel_bias, qt, kc, vct, ks, vst, kw, vwt, gate)
    y = _proj_ln(o.reshape(b * t, d), w_out.astype(BF16), x2, gain, bias, alpha)
    return y.reshape(b, t, d)


def _hgrn_kernel(lbp_ref, gain_ref, zq_ref, zf_ref, zi_ref, zg_ref, o_ref, st_ref, *, layer):
    C = HGRN_CHUNK
    SB = HGRN_SUB
    t = zq_ref.shape[1]

    p = lbp_ref[...]
    e = jnp.exp(p - jnp.max(p, axis=0, keepdims=True))
    sm = e / jnp.sum(e, axis=0, keepdims=True)
    cs = sm[0:1]
    for r in range(1, layer + 1):
        cs = cs + sm[r:r + 1]
    lb = cs - sm[0:1]
    log_lb = jnp.log(lb)
    log_1m = jnp.log1p(-lb)
    gain = gain_ref[...]

    st_ref[...] = jnp.zeros(st_ref.shape, F32)
    tri = jnp.where(lax.broadcasted_iota(jnp.int32, (C, C), 0) >= lax.broadcasted_iota(jnp.int32, (C, C), 1),
                    1.0, 0.0).astype(BF16)
    srow = lax.broadcasted_iota(jnp.int32, (C, 1), 0)
    trow = lax.broadcasted_iota(jnp.int32, (SB, 1), 0)
    lane = lax.broadcasted_iota(jnp.int32, (SB, C), 1)

    def chunk(c, carry):
        r0 = pl.multiple_of(c * C, C)
        zq = zq_ref[0, pl.ds(r0, C), :]
        zf = zf_ref[0, pl.ds(r0, C), :]
        v = zi_ref[0, pl.ds(r0, C), :]
        zg = zg_ref[0, pl.ds(r0, C), :]
        q = jax.nn.silu(zq)
        log_f = jnp.logaddexp(log_lb, log_1m + jax.nn.log_sigmoid(zf))
        kk = (1.0 - lb) * jax.nn.sigmoid(-zf)
        vb = v.astype(BF16)

        g_hi, g_mid, g_lo = _split3(log_f)
        bcum = _dot(tri, g_hi) + _dot(tri, g_mid) + _dot(tri, g_lo)
        b_last = bcum[C - 1:C, :]

        st = st_ref[...]
        o = _dot_nt((q * jnp.exp(bcum)).astype(BF16), st.astype(BF16))

        parts = []
        for sb in range(C // SB):
            lo = sb * SB
            b_i = bcum[lo:lo + SB]
            q_i = q[lo:lo + SB]
            if sb > 0:
                ref = bcum[lo - 1:lo, :]
                q_side = q_i * jnp.exp(b_i - ref)
                k_side = jnp.where(srow < lo, kk * jnp.exp(jnp.minimum(ref - bcum, 0.0)), 0.0)
                a = _dot_nt(q_side.astype(BF16), k_side.astype(BF16))
            else:
                a = jnp.zeros((SB, C), F32)
            for s in range(SB):
                w = jnp.where(trow >= s, jnp.exp(jnp.minimum(b_i - b_i[s:s + 1], 0.0)), 0.0)
                col = jnp.sum(q_i * kk[lo + s:lo + s + 1] * w, axis=-1, keepdims=True)
                a = jnp.where(lane == lo + s, col, a)
            parts.append(_dot(a.astype(BF16), vb))
        o = o + jnp.concatenate(parts, axis=0)

        kd = kk * jnp.exp(b_last - bcum)
        st_ref[...] = st * jnp.exp(b_last) + _dot(v.T.astype(BF16), kd.astype(BF16))

        o = o * lax.rsqrt(jnp.mean(o * o, axis=-1, keepdims=True) + RMS_EPS) * gain
        o = o * jax.nn.silu(zg)
        o_ref[0, pl.ds(r0, C), :] = o.astype(o_ref.dtype)
        return carry

    lax.fori_loop(0, t // C, chunk, 0)


def _hgrn_layer(x, w_in, w_out, norm_gain, lb_param, layer, gain, bias, alpha):
    b, t, d = x.shape
    nh = HGRN_HEADS
    kd = d // nh
    x2 = x.reshape(b * t, d)
    h = _proj(x2, w_in.astype(BF16), tn=1024).reshape(b, t, 4 * d)

    def zspec(part):
        return pl.BlockSpec((1, t, kd), lambda i, j: (i, 0, part * nh + j))

    o = pl.pallas_call(
        functools.partial(_hgrn_kernel, layer=layer),
        grid=(b, nh),
        in_specs=[pl.BlockSpec((lb_param.shape[0], kd), lambda i, j: (0, j)),
                  pl.BlockSpec((1, kd), lambda i, j: (0, 0)),
                  zspec(0), zspec(1), zspec(2), zspec(3)],
        out_specs=pl.BlockSpec((1, t, kd), lambda i, j: (i, 0, j)),
        out_shape=jax.ShapeDtypeStruct((b, t, d), BF16),
        scratch_shapes=[pltpu.VMEM((kd, kd), F32)],
        compiler_params=_cparams(("parallel", "parallel")),
        name="hgrn",
    )(lb_param, norm_gain.reshape(1, kd), h, h, h, h)
    y = _proj_ln(o.reshape(b * t, d), w_out.astype(BF16), x2, gain, bias, alpha)
    return y.reshape(b, t, d)


def _swa_kernel(tab_ref, sink_ref, q_ref, k_ref, v_ref, o_ref, bd_ref, bp_ref):
    L = SWA_WINDOW
    nq = SWA_QPG
    dh = HEAD_DIM
    g = pl.program_id(1)
    n = pl.program_id(2)
    ii = lax.broadcasted_iota(jnp.int32, (L, L), 0)
    jj = lax.broadcasted_iota(jnp.int32, (L, L), 1)

    @pl.when(n == 0)
    def _init():
        b_diag = _t5_bucket(ii - jj)
        b_prev = _t5_bucket(ii - jj + L)
        for h in range(nq):
            bd_ref[h] = _bias_lookup(b_diag, tab_ref, g * nq + h)
            bp_ref[h] = _bias_lookup(b_prev, tab_ref, g * nq + h)

    qs = (q_ref[0].astype(F32) * (dh ** -0.5)).astype(BF16).reshape(nq * L, dh)
    cur = pl.multiple_of(n * L, L)
    prev = pl.multiple_of(jnp.maximum(n - 1, 0) * L, L)
    k_c = k_ref[0, 0, pl.ds(cur, L), :]
    v_c = v_ref[0, 0, pl.ds(cur, L), :]
    k_p = k_ref[0, 0, pl.ds(prev, L), :]
    v_p = v_ref[0, 0, pl.ds(prev, L), :]

    s_c = _dot_nt(qs, k_c).reshape(nq, L, L) + bd_ref[...]
    s_p = _dot_nt(qs, k_p).reshape(nq, L, L) + bp_ref[...]
    s_c = jnp.where((ii >= jj)[None], s_c, NEG_INF)
    s_p = jnp.where(((jj > ii) & (n > 0))[None], s_p, NEG_INF)
    sink = _per_head_scalar(nq, lambda h: sink_ref[g * nq + h])
    m = jnp.maximum(jnp.maximum(jnp.max(s_c, axis=-1, keepdims=True), jnp.max(s_p, axis=-1, keepdims=True)), sink)
    e_c = jnp.exp(s_c - m)
    e_p = jnp.exp(s_p - m)
    den = jnp.sum(e_c, axis=-1, keepdims=True) + jnp.sum(e_p, axis=-1, keepdims=True) + jnp.exp(sink - m)
    pv = (_dot(e_c.reshape(nq * L, L).astype(BF16), v_c) + _dot(e_p.reshape(nq * L, L).astype(BF16), v_p))
    o = pv.reshape(nq, L, dh) / den
    o_ref[0] = jnp.concatenate([o[h] for h in range(nq)], axis=1).astype(o_ref.dtype)


def _swa_layer(x, w_in, w_out, sinks, rel_bias, gain, bias, alpha):
    b, t, d = x.shape
    kvh, nq, dh, L = SWA_KV_HEADS, SWA_QPG, HEAD_DIM, SWA_WINDOW
    assert t % L == 0 and L >= REL_MAX_DIST
    x2 = x.reshape(b * t, d)
    n_in = w_in.shape[1]
    h = _proj(x2, w_in.astype(BF16), tn=n_in // 2).reshape(b, t, n_in)

    def heads(lo, n_heads):
        z = h[:, :, lo:lo + n_heads * dh].reshape(b, t, n_heads, dh)
        return z.transpose(0, 2, 1, 3).astype(BF16)

    q = heads(0, ATTN_HEADS)
    k = heads(ATTN_HEADS * dh, kvh)
    v = heads(ATTN_HEADS * dh + kvh * dh, kvh)
    kvspec = pl.BlockSpec((1, 1, t, dh), lambda i, j, n: (i, j, 0, 0))
    o = pl.pallas_call(
        _swa_kernel,
        grid=(b, kvh, t // L),
        in_specs=[pl.BlockSpec(memory_space=pltpu.SMEM),
                  pl.BlockSpec(memory_space=pltpu.SMEM),
                  pl.BlockSpec((1, nq, L, dh), lambda i, j, n: (i, j, n, 0)),
                  kvspec, kvspec],
        out_specs=pl.BlockSpec((1, L, nq * dh), lambda i, j, n: (i, n, j)),
        out_shape=jax.ShapeDtypeStruct((b, t, d), BF16),
        scratch_shapes=[pltpu.VMEM((nq, L, L), F32), pltpu.VMEM((nq, L, L), F32)],
        compiler_params=_cparams(("parallel", "parallel", "arbitrary")),
        name="swa_attn",
    )(rel_bias, sinks, q, k, v)
    y = _proj_ln(o.reshape(b * t, d), w_out.astype(BF16), x2, gain, bias, alpha)
    return y.reshape(b, t, d)


def kernel(x, rel_bias, ln_gain, ln_bias, ffn1_w_gate, ffn1_w_up, ffn1_w_down, ffn2_w_gate, ffn2_w_up,
           ffn2_w_down, nsa_w_in, nsa_w_out, nsa_cmp_pos, nsa_cmp_w1, nsa_cmp_w2, hgrn_w_in, hgrn_w_out,
           hgrn_norm_gain, hgrn_lb, swa_w_in, swa_w_out, swa_sinks):
    depth = ln_gain.shape[0]
    alpha = (2.0 * depth) ** 0.25
    b, t, d = x.shape

    def ffn(x, wg, wu, wd, gain, bias):
        y = _ffn_ln(x.reshape(b * t, d), wg.astype(BF16), wu.astype(BF16), wd.astype(BF16), gain, bias, alpha)
        return y.reshape(b, t, d)

    for i in range(depth):
        x = ffn(x, ffn1_w_gate[i], ffn1_w_up[i], ffn1_w_down[i], ln_gain[i, 0], ln_bias[i, 0])
        kind, slot = i % N_MIXERS, i // N_MIXERS
        if kind == 0:
            x = _nsa_layer(x, nsa_w_in[slot], nsa_w_out[slot], nsa_cmp_pos[slot], nsa_cmp_w1[slot],
                           nsa_cmp_w2[slot], rel_bias, ln_gain[i, 1], ln_bias[i, 1], alpha)
        elif kind == 1:
            x = _hgrn_layer(x, hgrn_w_in[slot], hgrn_w_out[slot], hgrn_norm_gain[slot], hgrn_lb, i,
                            ln_gain[i, 1], ln_bias[i, 1], alpha)
        else:
            x = _swa_layer(x, swa_w_in[slot], swa_w_out[slot], swa_sinks[slot], rel_bias,
                           ln_gain[i, 1], ln_bias[i, 1], alpha)
        x = ffn(x, ffn2_w_gate[i], ffn2_w_up[i], ffn2_w_down[i], ln_gain[i, 2], ln_bias[i, 2])
    return x
```

```python
import functools
import math

import jax
import jax.numpy as jnp
import numpy as np
from jax import lax
from jax.experimental import pallas as pl
from jax.experimental.pallas import tpu as pltpu

F32 = jnp.float32
BF16 = jnp.bfloat16

DEPTH = 4
N_MIXERS = 3
REL_BUCKETS = 32
REL_EXACT = REL_BUCKETS // 2
REL_MAX_DIST = 128
ATTN_HEADS = 16
HEAD_DIM = 64
NSA_KV_HEADS = 4
NSA_QPG = ATTN_HEADS // NSA_KV_HEADS
CMP_STRIDE = 16
CMP_LEN = 2 * CMP_STRIDE
SEL_BLOCK = 64
SEL_COUNT = 16
NSA_WINDOW = 512
HGRN_HEADS = 8
HGRN_CHUNK = 64
HGRN_TC = 128
SWA_KV_HEADS = 2
SWA_QPG = ATTN_HEADS // SWA_KV_HEADS
SWA_WINDOW = 128
LN_EPS = 1e-5
RMS_EPS = 1e-6
NEG_INF = -1e30

LANE = 128
VMEM_LIMIT = 48 * 1024 * 1024
NSA_TQ = 256
NSA_TKF = 512
NSA_NEAR = NSA_TQ // CMP_STRIDE + 8


def _cparams(sem):
    return pltpu.CompilerParams(dimension_semantics=sem, vmem_limit_bytes=VMEM_LIMIT)


def _dot(a, b):
    return jnp.dot(a, b, preferred_element_type=F32)


def _dot_nt(a, b):
    return lax.dot_general(a, b, (((1,), (1,)), ((), ())), preferred_element_type=F32)


def _layer_norm_rows(y, g, b):
    mu = jnp.mean(y, axis=-1, keepdims=True)
    yc = y - mu
    var = jnp.mean(yc * yc, axis=-1, keepdims=True)
    return yc * lax.rsqrt(var + LN_EPS) * g + b


def _ffn_ln_kernel(x_ref, wg_ref, wu_ref, wd_ref, g_ref, b_ref, o_ref, xb_ref, acc_ref, *, alpha):
    j = pl.program_id(1)

    @pl.when(j == 0)
    def _():
        xb_ref[...] = x_ref[...].astype(BF16)
        acc_ref[...] = jnp.zeros_like(acc_ref)

    xb = xb_ref[...]
    gate = _dot(xb, wg_ref[...])
    up = _dot(xb, wu_ref[...])
    h = (jax.nn.silu(gate) * up).astype(BF16)
    acc_ref[...] += _dot(h, wd_ref[...])

    @pl.when(j == pl.num_programs(1) - 1)
    def _():
        y = alpha * x_ref[...] + 0.5 * acc_ref[...]
        o_ref[...] = _layer_norm_rows(y, g_ref[...], b_ref[...])


def _ffn_ln(x, wg, wu, wd, gain, bias, alpha, tm=512, tf=1408):
    n, d = x.shape
    f = wg.shape[1]
    tm = min(tm, n)
    assert n % tm == 0 and f % tf == 0
    return pl.pallas_call(
        functools.partial(_ffn_ln_kernel, alpha=alpha),
        grid=(n // tm, f // tf),
        in_specs=[
            pl.BlockSpec((tm, d), lambda i, j: (i, 0)),
            pl.BlockSpec((d, tf), lambda i, j: (0, j)),
            pl.BlockSpec((d, tf), lambda i, j: (0, j)),
            pl.BlockSpec((tf, d), lambda i, j: (j, 0)),
            pl.BlockSpec((1, d), lambda i, j: (0, 0)),
            pl.BlockSpec((1, d), lambda i, j: (0, 0)),
        ],
        out_specs=pl.BlockSpec((tm, d), lambda i, j: (i, 0)),
        out_shape=jax.ShapeDtypeStruct((n, d), F32),
        scratch_shapes=[pltpu.VMEM((tm, d), BF16), pltpu.VMEM((tm, d), F32)],
        compiler_params=_cparams(("parallel", "arbitrary")),
        name="ffn_ln",
    )(x, wg, wu, wd, gain.reshape(1, d), bias.reshape(1, d))


def _proj_kernel(x_ref, w_ref, o_ref):
    o_ref[...] = _dot(x_ref[...].astype(BF16), w_ref[...]).astype(o_ref.dtype)


def _proj(x, w, tn, tm=1024, out_dtype=F32):
    n, k = x.shape
    m = w.shape[1]
    tm = min(tm, n)
    assert n % tm == 0 and m % tn == 0
    return pl.pallas_call(
        _proj_kernel,
        grid=(n // tm, m // tn),
        in_specs=[pl.BlockSpec((tm, k), lambda i, j: (i, 0)),
                  pl.BlockSpec((k, tn), lambda i, j: (0, j))],
        out_specs=pl.BlockSpec((tm, tn), lambda i, j: (i, j)),
        out_shape=jax.ShapeDtypeStruct((n, m), out_dtype),
        compiler_params=_cparams(("parallel", "arbitrary")),
        name="proj",
    )(x, w)


def _proj_ln_kernel(a_ref, w_ref, r_ref, g_ref, b_ref, o_ref, *, alpha):
    y = _dot(a_ref[...].astype(BF16), w_ref[...])
    o_ref[...] = _layer_norm_rows(alpha * r_ref[...] + y, g_ref[...], b_ref[...])


def _proj_ln(a, w, res, gain, bias, alpha, tm=512):
    n, k = a.shape
    d = w.shape[1]
    tm = min(tm, n)
    assert n % tm == 0
    return pl.pallas_call(
        functools.partial(_proj_ln_kernel, alpha=alpha),
        grid=(n // tm,),
        in_specs=[pl.BlockSpec((tm, k), lambda i: (i, 0)),
                  pl.BlockSpec((k, d), lambda i: (0, 0)),
                  pl.BlockSpec((tm, d), lambda i: (i, 0)),
                  pl.BlockSpec((1, d), lambda i: (0, 0)),
                  pl.BlockSpec((1, d), lambda i: (0, 0))],
        out_specs=pl.BlockSpec((tm, d), lambda i: (i, 0)),
        out_shape=jax.ShapeDtypeStruct((n, d), F32),
        compiler_params=_cparams(("parallel",)),
        name="proj_ln",
    )(a, w, res, gain.reshape(1, d), bias.reshape(1, d))


def _t5_bucket(dist):
    n = jnp.maximum(dist, 0)
    nf = jnp.maximum(n, 1).astype(F32)
    large = REL_EXACT + (jnp.log(nf / REL_EXACT) / math.log(REL_MAX_DIST / REL_EXACT)
                         * (REL_BUCKETS - REL_EXACT)).astype(jnp.int32)
    return jnp.where(n < REL_EXACT, n, jnp.minimum(large, REL_BUCKETS - 1))


def _bias_lookup(bucket, tab_ref, col):
    out = jnp.zeros(bucket.shape, F32)
    for k in range(REL_BUCKETS):
        out = jnp.where(bucket == k, tab_ref[k, col], out)
    return out


def _per_head_scalar(n_heads, fn):
    hidx = lax.broadcasted_iota(jnp.int32, (n_heads, 1, 1), 0)
    out = jnp.zeros((n_heads, 1, 1), F32)
    for h in range(n_heads):
        out = jnp.where(hidx == h, fn(h), out)
    return out


def _split3(x):
    hi = x.astype(BF16)
    r1 = x - hi.astype(F32)
    mid = r1.astype(BF16)
    lo = (r1 - mid.astype(F32)).astype(BF16)
    return hi, mid, lo


def _nsa_compress_kernel(zk_ref, zv_ref, pos_ref, w1_ref, w2_ref, w2t_ref, kc_ref, vct_ref):
    def hidden(idx, z_ref):
        z = z_ref[0, 0]
        u = _dot((z + pos_ref[idx, 0]).astype(BF16), w1_ref[idx, 0])
        v = _dot((z + pos_ref[idx, 1]).astype(BF16), w1_ref[idx, 1])
        return jax.nn.gelu(u + pltpu.roll(v, z.shape[0] - 1, 0)).astype(BF16)

    kc_ref[0, 0] = _dot(hidden(0, zk_ref), w2_ref[0]).astype(kc_ref.dtype)
    vct_ref[0, 0] = _dot_nt(w2t_ref[1], hidden(1, zv_ref)).astype(vct_ref.dtype)


def _nsa_compress(zk, zv, pos, w1, w2):
    b, g, nc, kd = zk.shape
    hid = w1.shape[-1]
    dh = w2.shape[-1]
    zspec = pl.BlockSpec((1, 1, nc, kd), lambda i, j: (i, j, 0, 0))
    return pl.pallas_call(
        _nsa_compress_kernel,
        grid=(b, g),
        in_specs=[zspec, zspec,
                  pl.BlockSpec((2, 2, 1, kd), lambda i, j: (0, 0, 0, 0)),
                  pl.BlockSpec((2, 2, kd, hid), lambda i, j: (0, 0, 0, 0)),
                  pl.BlockSpec((2, hid, dh), lambda i, j: (0, 0, 0)),
                  pl.BlockSpec((2, dh, hid), lambda i, j: (0, 0, 0))],
        out_specs=[pl.BlockSpec((1, 1, nc, dh), lambda i, j: (i, j, 0, 0)),
                   pl.BlockSpec((1, 1, dh, nc), lambda i, j: (i, j, 0, 0))],
        out_shape=[jax.ShapeDtypeStruct((b, g, nc, dh), BF16), jax.ShapeDtypeStruct((b, g, dh, nc), BF16)],
        compiler_params=_cparams(("parallel", "parallel")),
        name="nsa_compress",
    )(zk, zv, pos, w1, w2, w2.transpose(0, 2, 1))


def _nsa_attn_kernel(tab_ref, qt_ref, kc_ref, vct_ref, ks_ref, vst_ref, kw_ref, vwt_ref, gate_ref, o_ref,
                     bn_ref, edge_ref, cb_ref, vt_ref, neg_ref, add_ref, m_ref, acc_ref, *, n_sel):
    tq = NSA_TQ
    tkf = NSA_TKF
    nq = NSA_QPG
    dh = HEAD_DIM
    w = nq * tq
    g = pl.program_id(1)
    iq = pl.program_id(2)
    t0 = iq * tq
    n_cmp = kc_ref.shape[2]
    n_blk = vt_ref.shape[0]
    bpt = tq // SEL_BLOCK
    far_bucket = REL_BUCKETS - 1

    def head_cols(fn):
        return jnp.concatenate([fn(h) for h in range(nq)], axis=1)

    far_row = head_cols(lambda h: jnp.full((1, tq), tab_ref[far_bucket, g * nq + h], F32))

    @pl.when(iq == 0)
    def _init():
        jj = lax.broadcasted_iota(jnp.int32, (tq, tq), 0)
        ii = lax.broadcasted_iota(jnp.int32, (tq, tq), 1)
        d = ii - jj
        b_diag = _t5_bucket(d)
        b_prev = _t5_bucket(d + tq)
        mm = lax.broadcasted_iota(jnp.int32, (LANE, tq), 0)
        i2 = lax.broadcasted_iota(jnp.int32, (LANE, tq), 1)
        dn = i2 - CMP_STRIDE * mm + (9 * CMP_STRIDE - CMP_LEN + 1)
        b_near = _t5_bucket(dn)
        for h in range(nq):
            col = g * nq + h
            far = tab_ref[far_bucket, col]
            cols = slice(h * tq, (h + 1) * tq)
            bn_ref[0:tq, cols] = _bias_lookup(b_prev, tab_ref, col)
            bn_ref[tq:2 * tq, cols] = jnp.where(d >= 0, _bias_lookup(b_diag, tab_ref, col), NEG_INF)
            edge_ref[:, cols] = jnp.where(jj > ii, far, NEG_INF)
            near = jnp.where(dn >= 0, _bias_lookup(b_near, tab_ref, col), NEG_INF)
            cbv = jnp.where(mm < NSA_NEAR, near,
                            jnp.where(mm == NSA_NEAR, far, jnp.where(mm == NSA_NEAR + 1, NEG_INF, 0.0)))
            hi, mid, lo = _split3(cbv)
            cb_ref[0, :, cols] = hi
            cb_ref[1, :, cols] = mid
            cb_ref[2, :, cols] = lo

    qt = (head_cols(lambda h: qt_ref[0, h]).astype(F32) * (dh ** -0.5)).astype(BF16)

    n_lo = t0 // CMP_STRIDE - 9
    ni = lax.broadcasted_iota(jnp.int32, (n_cmp, LANE), 0)
    mi = lax.broadcasted_iota(jnp.int32, (n_cmp, LANE), 1)
    place = (((mi < NSA_NEAR) & (ni == mi + n_lo)) | ((mi == NSA_NEAR) & (ni < n_lo))
             | ((mi == NSA_NEAR + 1) & (ni >= n_lo + NSA_NEAR)))
    shift = jnp.where(place, 1.0, 0.0).astype(BF16)
    lcm = (_dot(kc_ref[0, 0], qt) + _dot(shift, cb_ref[0]) + _dot(shift, cb_ref[1]) + _dot(shift, cb_ref[2]))
    ec = jnp.exp(lcm - jnp.max(lcm, axis=0, keepdims=True))
    sc = jnp.sum(ec, axis=0, keepdims=True)
    tl = t0 + lax.broadcasted_iota(jnp.int32, (1, w), 1) % tq
    pct = ec * jnp.where(tl >= CMP_LEN - 1, 1.0 / sc, 0.0)
    o_ct = _dot(vct_ref[0, 0], pct.astype(BF16))

    psum = pct[:, 0:tq]
    for h in range(1, nq):
        psum = psum + pct[:, h * tq:(h + 1) * tq]
    p_hi = psum.astype(BF16)
    p_lo = (psum - p_hi.astype(F32)).astype(BF16)
    jb = lax.broadcasted_iota(jnp.int32, (n_blk, n_cmp), 0)
    nb = lax.broadcasted_iota(jnp.int32, (n_blk, n_cmp), 1)
    ovl = jnp.where((CMP_STRIDE * nb < SEL_BLOCK * (jb + 1)) & (CMP_STRIDE * nb + CMP_LEN > SEL_BLOCK * jb),
                    1.0, 0.0).astype(BF16)
    imp_t = _dot(ovl, p_hi) + _dot(ovl, p_lo)
    jb2 = lax.broadcasted_iota(jnp.int32, (n_blk, tq), 0)
    cur = (t0 + lax.broadcasted_iota(jnp.int32, (n_blk, tq), 1)) // SEL_BLOCK
    forced = (jb2 == 0) | (jb2 == cur) | (jb2 == cur - 1)
    val = jnp.where(jb2 > cur, -1e9, jnp.where(forced, 1e9, imp_t))
    vt_ref[...] = val

    def rank_body(jp, rank):
        row = vt_ref[pl.ds(jp, 1), :]
        beats = (row > val) | ((row == val) & (jp < jb2))
        return rank + jnp.where(beats, 1, 0)

    n_causal = (t0 + tq) // SEL_BLOCK
    rank = lax.fori_loop(0, jnp.where(n_causal <= n_sel, 0, n_causal), rank_body,
                         jnp.zeros((n_blk, tq), jnp.int32))
    neg_t = jnp.where(rank < n_sel, 0.0, NEG_INF)
    neg4 = jnp.concatenate([neg_t] * nq, axis=1)
    neg_ref[...] = neg4

    blk0 = iq * bpt
    n_far = jnp.maximum(blk0 - bpt, 0)
    bpf = tkf // SEL_BLOCK
    brow = lax.broadcasted_iota(jnp.int32, (n_blk, 1), 0)
    far_add = jnp.where(brow < n_far, neg4 + far_row, NEG_INF).reshape(n_blk // bpf, bpf, w)
    pad_rows = jnp.where(lax.broadcasted_iota(jnp.int32, (1, 16 - bpf, 1), 1) == 0, NEG_INF, 0.0)
    far_add = jnp.concatenate([far_add, jnp.broadcast_to(pad_rows, (n_blk // bpf, 16 - bpf, w))], axis=1)
    for part, piece in enumerate(_split3(far_add)):
        add_ref[part] = piece
    zero_rows = jnp.zeros((dh - 48, w), BF16)

    def aug_rhs(c):
        return jnp.concatenate([qt, add_ref[0, c], add_ref[1, c], add_ref[2, c], zero_rows], axis=0)

    pad_only = jnp.where(lax.broadcasted_iota(jnp.int32, (16, 1), 0) == bpf, NEG_INF, 0.0) + jnp.zeros((16, w), F32)
    rhs_plain = jnp.concatenate([qt] + list(_split3(pad_only)) + [zero_rows], axis=0)

    def kv_tile(k_ref, v_ref, key0, n_keys):
        start = pl.multiple_of(key0 + tq, LANE)
        return k_ref[0, 0, pl.ds(start, n_keys), :], v_ref[0, 0, :, pl.ds(start, n_keys)]

    def add_block_rows(s, rows):
        return jnp.concatenate([s[r * SEL_BLOCK:(r + 1) * SEL_BLOCK] + row for r, row in enumerate(rows)], axis=0)

    def flash_first(s, vt):
        m = jnp.max(s, axis=0, keepdims=True)
        m_ref[...] = m
        acc_ref[...] = _dot(vt, jnp.exp(s - m).astype(BF16))

    def flash_next(s, vt):
        m_old = m_ref[...]
        m_new = jnp.maximum(m_old, jnp.max(s, axis=0, keepdims=True))
        acc_ref[...] = jnp.exp(m_old - m_new) * acc_ref[...] + _dot(vt, jnp.exp(s - m_new).astype(BF16))
        m_ref[...] = m_new

    def flash_out():
        acc = acc_ref[...]
        return acc[0:dh] * (1.0 / acc[dh:dh + 1])

    k, vt = kv_tile(ks_ref, vst_ref, t0 - tq, 2 * tq)
    s = _dot(k, rhs_plain) + bn_ref[...]
    near_rows = [neg_ref[pl.ds(jnp.maximum(blk0 - bpt + r, 0), 1), :] for r in range(2 * bpt)]
    flash_first(add_block_rows(s, near_rows), vt)

    def far_body(c, carry):
        k, vt = kv_tile(ks_ref, vst_ref, c * tkf, tkf)
        flash_next(_dot(k, aug_rhs(c)), vt)
        return carry

    lax.fori_loop(0, (n_far + bpf - 1) // bpf, far_body, 0)
    o_st = flash_out()

    k, vt = kv_tile(kw_ref, vwt_ref, t0 - tq, 2 * tq)
    flash_first(_dot(k, rhs_plain) + bn_ref[...], vt)

    @pl.when(iq >= 2)
    def _():
        k, vt = kv_tile(kw_ref, vwt_ref, t0 - 2 * tq, tq)
        flash_next(_dot(k, rhs_plain) + edge_ref[...], vt)

    o_wt = flash_out()

    sg = jax.nn.sigmoid(gate_ref[0, 0])
    outs = []
    for h in range(nq):
        cols = slice(h * tq, (h + 1) * tq)
        outs.append(sg[3 * h:3 * h + 1] * o_ct[:, cols] + sg[3 * h + 1:3 * h + 2] * o_st[:, cols]
                    + sg[3 * h + 2:3 * h + 3] * o_wt[:, cols])
    o_ref[0] = jnp.concatenate(outs, axis=0).T.astype(o_ref.dtype)


def _nsa_attention(tab, qt, kc, vct, ks, vst, kw, vwt, gate):
    b, _, dh, t = qt.shape
    g = NSA_KV_HEADS
    nq = NSA_QPG
    tq = NSA_TQ
    w = nq * tq
    assert t % tq == 0 and tq >= REL_MAX_DIST and NSA_WINDOW == 2 * tq and tq % SEL_BLOCK == 0
    assert NSA_TKF % SEL_BLOCK == 0 and 2 * tq >= NSA_TKF and t >= NSA_TKF and tq % LANE == 0
    assert NSA_NEAR + 1 < LANE
    n_cmp = kc.shape[2]
    n_blk = t // SEL_BLOCK
    n_sel = min(SEL_COUNT, n_blk)
    assert n_sel >= 3
    tp = ks.shape[2]
    bpf = NSA_TKF // SEL_BLOCK
    assert tp == t + tq and vst.shape[3] == tp and bpf < 16 and n_blk % bpf == 0 and 2 * tq // SEL_BLOCK == bpf
    assert ks.shape[3] == 2 * dh and vst.shape[2] == dh + 16
    kspec = pl.BlockSpec((1, 1, tp, 2 * dh), lambda i, j, k: (i, j, 0, 0))
    vtspec = pl.BlockSpec((1, 1, dh + 16, tp), lambda i, j, k: (i, j, 0, 0))
    return pl.pallas_call(
        functools.partial(_nsa_attn_kernel, n_sel=n_sel),
        grid=(b, g, t // tq),
        in_specs=[pl.BlockSpec(memory_space=pltpu.SMEM),
                  pl.BlockSpec((1, nq, dh, tq), lambda i, j, k: (i, j, 0, k)),
                  pl.BlockSpec((1, 1, n_cmp, dh), lambda i, j, k: (i, j, 0, 0)),
                  pl.BlockSpec((1, 1, dh, n_cmp), lambda i, j, k: (i, j, 0, 0)),
                  kspec, vtspec, kspec, vtspec,
                  pl.BlockSpec((1, 1, 16, tq), lambda i, j, k: (i, j, 0, k))],
        out_specs=pl.BlockSpec((1, tq, nq * dh), lambda i, j, k: (i, k, j)),
        out_shape=jax.ShapeDtypeStruct((b, t, g * nq * dh), BF16),
        scratch_shapes=[pltpu.VMEM((2 * tq, w), F32),
                        pltpu.VMEM((tq, w), F32),
                        pltpu.VMEM((3, LANE, w), BF16),
                        pltpu.VMEM((n_blk, tq), F32),
                        pltpu.VMEM((n_blk, w), F32),
                        pltpu.VMEM((3, n_blk // bpf, 16, w), BF16),
                        pltpu.VMEM((1, w), F32),
                        pltpu.VMEM((dh + 16, w), F32)],
        compiler_params=_cparams(("parallel", "parallel", "arbitrary")),
        name="nsa_attn",
    )(tab, qt, kc, vct, ks, vst, kw, vwt, gate)


def _nsa_layer(x, w_in, w_out, cmp_pos, cmp_w1, cmp_w2, rel_bias, gain, bias, alpha):
    b, t, d = x.shape
    g, nq, dh = NSA_KV_HEADS, NSA_QPG, HEAD_DIM
    kvw = g * dh
    n_in = w_in.shape[1]
    n_pad = -n_in % (3 * LANE)
    w_in_p = jnp.pad(w_in, ((0, 0), (0, n_pad))).astype(BF16)
    x2 = x.reshape(b * t, d)
    h = _proj(x2, w_in_p, tn=3 * LANE).reshape(b, t, n_in + n_pad)

    def heads(lo, n_heads, dt, perm=(0, 2, 1, 3)):
        z = h[:, :, lo:lo + n_heads * dh].reshape(b, t, n_heads, dh)
        return z.transpose(perm).astype(dt)

    head_t = (0, 2, 3, 1)
    qt = heads(0, ATTN_HEADS, BF16, head_t)
    off = ATTN_HEADS * dh
    zk = heads(off, g, F32).reshape(b, g, t // CMP_STRIDE, CMP_STRIDE * dh)
    zv = heads(off + kvw, g, F32).reshape(b, g, t // CMP_STRIDE, CMP_STRIDE * dh)
    tp = t + NSA_TQ
    slot = (np.arange(tp) - NSA_TQ) // SEL_BLOCK % (NSA_TKF // SEL_BLOCK)
    slot = np.where(np.arange(tp) < NSA_TQ, NSA_TKF // SEL_BLOCK, slot)
    onehot = (np.arange(16)[None, :] == slot[:, None]).astype(np.float32)
    k_cols = jnp.asarray(np.concatenate([onehot] * 3 + [np.zeros((tp, dh - 48), np.float32)], axis=1), BF16)
    v_rows = jnp.asarray((np.arange(16)[:, None] == 0) * np.ones((1, tp)), BF16)

    def aug_k(z):
        z = jnp.pad(z, ((0, 0), (0, 0), (NSA_TQ, 0), (0, 0)))
        return jnp.concatenate([z, jnp.broadcast_to(k_cols, (b, g, tp, dh))], axis=3)

    def aug_vt(z):
        z = jnp.pad(z, ((0, 0), (0, 0), (0, 0), (NSA_TQ, 0)))
        return jnp.concatenate([z, jnp.broadcast_to(v_rows, (b, g, 16, tp))], axis=2)

    ks = aug_k(heads(off + 2 * kvw, g, BF16))
    vst = aug_vt(heads(off + 3 * kvw, g, BF16, head_t))
    kw = aug_k(heads(off + 4 * kvw, g, BF16))
    vwt = aug_vt(heads(off + 5 * kvw, g, BF16, head_t))
    gate = h[:, :, off + 6 * kvw:off + 6 * kvw + 3 * ATTN_HEADS].reshape(b, t, g, 3 * nq)
    gate = jnp.pad(gate.transpose(0, 2, 3, 1), ((0, 0), (0, 0), (0, 16 - 3 * nq), (0, 0)))

    pos = cmp_pos.reshape(2, 2, 1, CMP_STRIDE * dh)
    w1 = cmp_w1.reshape(2, 2, CMP_STRIDE * dh, cmp_w1.shape[-1]).astype(BF16)
    kc, vct = _nsa_compress(zk, zv, pos, w1, cmp_w2.astype(BF16))
    o = _nsa_attention(rel_bias, qt, kc, vct, ks, vst, kw, vwt, gate)
    y = _proj_ln(o.reshape(b * t, d), w_out.astype(BF16), x2, gain, bias, alpha)
    return y.reshape(b, t, d)


def _hgrn_kernel(lbp_ref, gain_ref, zq_ref, zf_ref, zi_ref, zg_ref, o_ref, st_ref, *, layer):
    C = HGRN_TC
    t = zq_ref.shape[1]
    kdim = zq_ref.shape[2]

    p = lbp_ref[...]
    e = jnp.exp(p - jnp.max(p, axis=0, keepdims=True))
    sm = e / jnp.sum(e, axis=0, keepdims=True)
    cs = sm[0:1]
    for r in range(1, layer + 1):
        cs = cs + sm[r:r + 1]
    lb = cs - sm[0:1]
    log_lb = jnp.log(lb)
    log_1m = jnp.log1p(-lb)
    gain = gain_ref[...]

    st_ref[...] = jnp.zeros(st_ref.shape, F32)

    rr = lax.broadcasted_iota(jnp.int32, (C, C), 0)
    cc = lax.broadcasted_iota(jnp.int32, (C, C), 1)
    srow = lax.broadcasted_iota(jnp.int32, (C, 1), 0)
    halves = [1 << i for i in range(C.bit_length() - 1)]
    small = [h for h in halves if h < 8]
    sel_rows = [jnp.where(cc <= rr, 1.0, 0.0).astype(BF16)]
    sel_rows += [jnp.where(cc <= (rr // (2 * h)) * (2 * h) + h - 1, 1.0, 0.0).astype(BF16) for h in small]
    cum_sel = jnp.concatenate(sel_rows, axis=0)
    right = [(srow // h) % 2 == 1 for h in halves]
    same_blk = [rr // (2 * h) == cc // (2 * h) for h in halves]
    eye = rr == cc

    def chunk(c, carry):
        r0 = pl.multiple_of(c * C, C)
        zq = zq_ref[0, pl.ds(r0, C), :]
        zf = zf_ref[0, pl.ds(r0, C), :]
        v = zi_ref[0, pl.ds(r0, C), :]
        zg = zg_ref[0, pl.ds(r0, C), :]
        q = jax.nn.silu(zq)
        log_f = jnp.logaddexp(log_lb, log_1m + jax.nn.log_sigmoid(zf))
        kk = (1.0 - lb) * jax.nn.sigmoid(-zf)
        vb = v.astype(BF16)

        g_hi, g_mid, g_lo = _split3(log_f)
        cums = _dot(cum_sel, g_hi) + _dot(cum_sel, g_mid) + _dot(cum_sel, g_lo)
        bcum = cums[0:C]
        b_last = bcum[C - 1:C, :]

        st = st_ref[...]
        o = _dot_nt((q * jnp.exp(bcum)).astype(BF16), st.astype(BF16))

        a = jnp.where(eye, _dot_nt(q.astype(BF16), kk.astype(BF16)), 0.0)
        for lvl, h in enumerate(halves):
            if h in small:
                b_ref = cums[(lvl + 1) * C:(lvl + 2) * C]
            else:
                b_ref = jnp.broadcast_to(bcum.reshape(C // (2 * h), 2 * h, kdim)[:, h - 1:h, :],
                                         (C // (2 * h), 2 * h, kdim)).reshape(C, kdim)
            e = jnp.exp(-jnp.abs(bcum - b_ref))
            q_side = jnp.where(right[lvl], q * e, 0.0)
            k_side = jnp.where(right[lvl], 0.0, kk * e)
            a = a + jnp.where(same_blk[lvl], _dot_nt(q_side.astype(BF16), k_side.astype(BF16)), 0.0)
        o = o + _dot(a.astype(BF16), vb)

        kd = kk * jnp.exp(b_last - bcum)
        st_ref[...] = st * jnp.exp(b_last) + _dot(v.T.astype(BF16), kd.astype(BF16))

        o = o * lax.rsqrt(jnp.mean(o * o, axis=-1, keepdims=True) + RMS_EPS) * gain
        o = o * jax.nn.silu(zg)
        o_ref[0, pl.ds(r0, C), :] = o.astype(o_ref.dtype)
        return carry

    lax.fori_loop(0, t // C, chunk, 0, unroll=4)


def _hgrn_layer(x, w_in, w_out, norm_gain, lb_param, layer, gain, bias, alpha):
    b, t, d = x.shape
    nh = HGRN_HEADS
    kd = d // nh
    x2 = x.reshape(b * t, d)
    h = _proj(x2, w_in.astype(BF16), tn=1024).reshape(b, t, 4 * d)

    def zspec(part):
        return pl.BlockSpec((1, t, kd), lambda i, j: (i, 0, part * nh + j))

    o = pl.pallas_call(
        functools.partial(_hgrn_kernel, layer=layer),
        grid=(b, nh),
        in_specs=[pl.BlockSpec((lb_param.shape[0], kd), lambda i, j: (0, j)),
                  pl.BlockSpec((1, kd), lambda i, j: (0, 0)),
                  zspec(0), zspec(1), zspec(2), zspec(3)],
        out_specs=pl.BlockSpec((1, t, kd), lambda i, j: (i, 0, j)),
        out_shape=jax.ShapeDtypeStruct((b, t, d), BF16),
        scratch_shapes=[pltpu.VMEM((kd, kd), F32)],
        compiler_params=_cparams(("parallel", "parallel")),
        name="hgrn",
    )(lb_param, norm_gain.reshape(1, kd), h, h, h, h)
    y = _proj_ln(o.reshape(b * t, d), w_out.astype(BF16), x2, gain, bias, alpha)
    return y.reshape(b, t, d)


def _swa_kernel(tab_ref, sink_ref, q_ref, k_ref, v_ref, o_ref, bd_ref, bp_ref):
    L = SWA_WINDOW
    nq = SWA_QPG
    dh = HEAD_DIM
    g = pl.program_id(1)
    n = pl.program_id(2)
    ii = lax.broadcasted_iota(jnp.int32, (L, L), 0)
    jj = lax.broadcasted_iota(jnp.int32, (L, L), 1)

    @pl.when(n == 0)
    def _init():
        b_diag = _t5_bucket(ii - jj)
        b_prev = _t5_bucket(ii - jj + L)
        for h in range(nq):
            bd_ref[h] = _bias_lookup(b_diag, tab_ref, g * nq + h)
            bp_ref[h] = _bias_lookup(b_prev, tab_ref, g * nq + h)

    qs = (q_ref[0].astype(F32) * (dh ** -0.5)).astype(BF16).reshape(nq * L, dh)
    cur = pl.multiple_of(n * L, L)
    prev = pl.multiple_of(jnp.maximum(n - 1, 0) * L, L)
    k_c = k_ref[0, 0, pl.ds(cur, L), :]
    v_c = v_ref[0, 0, pl.ds(cur, L), :]
    k_p = k_ref[0, 0, pl.ds(prev, L), :]
    v_p = v_ref[0, 0, pl.ds(prev, L), :]

    s_c = _dot_nt(qs, k_c).reshape(nq, L, L) + bd_ref[...]
    s_p = _dot_nt(qs, k_p).reshape(nq, L, L) + bp_ref[...]
    s_c = jnp.where((ii >= jj)[None], s_c, NEG_INF)
    s_p = jnp.where(((jj > ii) & (n > 0))[None], s_p, NEG_INF)
    sink = _per_head_scalar(nq, lambda h: sink_ref[g * nq + h])
    m = jnp.maximum(jnp.maximum(jnp.max(s_c, axis=-1, keepdims=True), jnp.max(s_p, axis=-1, keepdims=True)), sink)
    e_c = jnp.exp(s_c - m)
    e_p = jnp.exp(s_p - m)
    den = jnp.sum(e_c, axis=-1, keepdims=True) + jnp.sum(e_p, axis=-1, keepdims=True) + jnp.exp(sink - m)
    pv = (_dot(e_c.reshape(nq * L, L).astype(BF16), v_c) + _dot(e_p.reshape(nq * L, L).astype(BF16), v_p))
    o = pv.reshape(nq, L, dh) / den
    o_ref[0] = jnp.concatenate([o[h] for h in range(nq)], axis=1).astype(o_ref.dtype)


def _swa_layer(x, w_in, w_out, sinks, rel_bias, gain, bias, alpha):
    b, t, d = x.shape
    kvh, nq, dh, L = SWA_KV_HEADS, SWA_QPG, HEAD_DIM, SWA_WINDOW
    assert t % L == 0 and L >= REL_MAX_DIST
    x2 = x.reshape(b * t, d)
    n_in = w_in.shape[1]
    h = _proj(x2, w_in.astype(BF16), tn=n_in // 2).reshape(b, t, n_in)

    def heads(lo, n_heads):
        z = h[:, :, lo:lo + n_heads * dh].reshape(b, t, n_heads, dh)
        return z.transpose(0, 2, 1, 3).astype(BF16)

    q = heads(0, ATTN_HEADS)
    k = heads(ATTN_HEADS * dh, kvh)
    v = heads(ATTN_HEADS * dh + kvh * dh, kvh)
    kvspec = pl.BlockSpec((1, 1, t, dh), lambda i, j, n: (i, j, 0, 0))
    o = pl.pallas_call(
        _swa_kernel,
        grid=(b, kvh, t // L),
        in_specs=[pl.BlockSpec(memory_space=pltpu.SMEM),
                  pl.BlockSpec(memory_space=pltpu.SMEM),
                  pl.BlockSpec((1, nq, L, dh), lambda i, j, n: (i, j, n, 0)),
                  kvspec, kvspec],
        out_specs=pl.BlockSpec((1, L, nq * dh), lambda i, j, n: (i, n, j)),
        out_shape=jax.ShapeDtypeStruct((b, t, d), BF16),
        scratch_shapes=[pltpu.VMEM((nq, L, L), F32), pltpu.VMEM((nq, L, L), F32)],
        compiler_params=_cparams(("parallel", "parallel", "arbitrary")),
        name="swa_attn",
    )(rel_bias, sinks, q, k, v)
    y = _proj_ln(o.reshape(b * t, d), w_out.astype(BF16), x2, gain, bias, alpha)
    return y.reshape(b, t, d)


def kernel(x, rel_bias, ln_gain, ln_bias, ffn1_w_gate, ffn1_w_up, ffn1_w_down, ffn2_w_gate, ffn2_w_up,
           ffn2_w_down, nsa_w_in, nsa_w_out, nsa_cmp_pos, nsa_cmp_w1, nsa_cmp_w2, hgrn_w_in, hgrn_w_out,
           hgrn_norm_gain, hgrn_lb, swa_w_in, swa_w_out, swa_sinks):
    depth = ln_gain.shape[0]
    alpha = (2.0 * depth) ** 0.25
    b, t, d = x.shape

    def ffn(x, wg, wu, wd, gain, bias):
        y = _ffn_ln(x.reshape(b * t, d), wg.astype(BF16), wu.astype(BF16), wd.astype(BF16), gain, bias, alpha)
        return y.reshape(b, t, d)

    for i in range(depth):
        x = ffn(x, ffn1_w_gate[i], ffn1_w_up[i], ffn1_w_down[i], ln_gain[i, 0], ln_bias[i, 0])
        kind, slot = i % N_MIXERS, i // N_MIXERS
        if kind == 0:
            x = _nsa_layer(x, nsa_w_in[slot], nsa_w_out[slot], nsa_cmp_pos[slot], nsa_cmp_w1[slot],
                           nsa_cmp_w2[slot], rel_bias, ln_gain[i, 1], ln_bias[i, 1], alpha)
        elif kind == 1:
            x = _hgrn_layer(x, hgrn_w_in[slot], hgrn_w_out[slot], hgrn_norm_gain[slot], hgrn_lb, i,
                            ln_gain[i, 1], ln_bias[i, 1], alpha)
        else:
            x = _swa_layer(x, swa_w_in[slot], swa_w_out[slot], swa_sinks[slot], rel_bias,
                           ln_gain[i, 1], ln_bias[i, 1], alpha)
        x = ffn(x, ffn2_w_gate[i], ffn2_w_up[i], ffn2_w_down[i], ln_gain[i, 2], ln_bias[i, 2])
    return x
```

```python
import functools
import math

import jax
import jax.numpy as jnp
import numpy as np
from jax import lax
from jax.experimental import pallas as pl
from jax.experimental.pallas import tpu as pltpu

F32 = jnp.float32
BF16 = jnp.bfloat16

DEPTH = 4
N_MIXERS = 3
REL_BUCKETS = 32
REL_EXACT = REL_BUCKETS // 2
REL_MAX_DIST = 128
ATTN_HEADS = 16
HEAD_DIM = 64
NSA_KV_HEADS = 4
NSA_QPG = ATTN_HEADS // NSA_KV_HEADS
CMP_STRIDE = 16
CMP_LEN = 2 * CMP_STRIDE
SEL_BLOCK = 64
SEL_COUNT = 16
NSA_WINDOW = 512
HGRN_HEADS = 8
HGRN_CHUNK = 64
HGRN_TC = 128
SWA_KV_HEADS = 2
SWA_QPG = ATTN_HEADS // SWA_KV_HEADS
SWA_WINDOW = 128
LN_EPS = 1e-5
RMS_EPS = 1e-6
NEG_INF = -1e30

LANE = 128
VMEM_LIMIT = 48 * 1024 * 1024
NSA_TQ = 256
NSA_TKF = 256
NSA_NEAR = NSA_TQ // CMP_STRIDE + 8
NSA_CB_ROWS = 32


def _cparams(sem):
    return pltpu.CompilerParams(dimension_semantics=sem, vmem_limit_bytes=VMEM_LIMIT)


def _dot(a, b):
    return jnp.dot(a, b, preferred_element_type=F32)


def _dot_nt(a, b):
    return lax.dot_general(a, b, (((1,), (1,)), ((), ())), preferred_element_type=F32)


def _layer_norm_rows(y, g, b):
    mu = jnp.mean(y, axis=-1, keepdims=True)
    yc = y - mu
    var = jnp.mean(yc * yc, axis=-1, keepdims=True)
    return yc * lax.rsqrt(var + LN_EPS) * g + b


def _ffn_ln_kernel(x_ref, wg_ref, wu_ref, wd_ref, g_ref, b_ref, o_ref, xb_ref, acc_ref, *, alpha):
    j = pl.program_id(1)

    @pl.when(j == 0)
    def _():
        xb_ref[...] = x_ref[...].astype(BF16)
        acc_ref[...] = jnp.zeros_like(acc_ref)

    xb = xb_ref[...]
    gate = _dot(xb, wg_ref[...])
    up = _dot(xb, wu_ref[...])
    h = (jax.nn.silu(gate) * up).astype(BF16)
    acc_ref[...] += _dot(h, wd_ref[...])

    @pl.when(j == pl.num_programs(1) - 1)
    def _():
        y = alpha * x_ref[...] + 0.5 * acc_ref[...]
        o_ref[...] = _layer_norm_rows(y, g_ref[...], b_ref[...])


def _ffn_ln(x, wg, wu, wd, gain, bias, alpha, tm=512, tf=1408):
    n, d = x.shape
    f = wg.shape[1]
    tm = min(tm, n)
    assert n % tm == 0 and f % tf == 0
    return pl.pallas_call(
        functools.partial(_ffn_ln_kernel, alpha=alpha),
        grid=(n // tm, f // tf),
        in_specs=[
            pl.BlockSpec((tm, d), lambda i, j: (i, 0)),
            pl.BlockSpec((d, tf), lambda i, j: (0, j)),
            pl.BlockSpec((d, tf), lambda i, j: (0, j)),
            pl.BlockSpec((tf, d), lambda i, j: (j, 0)),
            pl.BlockSpec((1, d), lambda i, j: (0, 0)),
            pl.BlockSpec((1, d), lambda i, j: (0, 0)),
        ],
        out_specs=pl.BlockSpec((tm, d), lambda i, j: (i, 0)),
        out_shape=jax.ShapeDtypeStruct((n, d), F32),
        scratch_shapes=[pltpu.VMEM((tm, d), BF16), pltpu.VMEM((tm, d), F32)],
        compiler_params=_cparams(("parallel", "arbitrary")),
        name="ffn_ln",
    )(x, wg, wu, wd, gain.reshape(1, d), bias.reshape(1, d))


def _proj_kernel(x_ref, w_ref, o_ref):
    o_ref[...] = _dot(x_ref[...].astype(BF16), w_ref[...]).astype(o_ref.dtype)


def _proj(x, w, tn, tm=1024, out_dtype=F32):
    n, k = x.shape
    m = w.shape[1]
    tm = min(tm, n)
    assert n % tm == 0 and m % tn == 0
    return pl.pallas_call(
        _proj_kernel,
        grid=(n // tm, m // tn),
        in_specs=[pl.BlockSpec((tm, k), lambda i, j: (i, 0)),
                  pl.BlockSpec((k, tn), lambda i, j: (0, j))],
        out_specs=pl.BlockSpec((tm, tn), lambda i, j: (i, j)),
        out_shape=jax.ShapeDtypeStruct((n, m), out_dtype),
        compiler_params=_cparams(("parallel", "arbitrary")),
        name="proj",
    )(x, w)


def _proj_ln_kernel(a_ref, w_ref, r_ref, g_ref, b_ref, o_ref, *, alpha):
    y = _dot(a_ref[...].astype(BF16), w_ref[...])
    o_ref[...] = _layer_norm_rows(alpha * r_ref[...] + y, g_ref[...], b_ref[...])


def _proj_ln(a, w, res, gain, bias, alpha, tm=512):
    n, k = a.shape
    d = w.shape[1]
    tm = min(tm, n)
    assert n % tm == 0
    return pl.pallas_call(
        functools.partial(_proj_ln_kernel, alpha=alpha),
        grid=(n // tm,),
        in_specs=[pl.BlockSpec((tm, k), lambda i: (i, 0)),
                  pl.BlockSpec((k, d), lambda i: (0, 0)),
                  pl.BlockSpec((tm, d), lambda i: (i, 0)),
                  pl.BlockSpec((1, d), lambda i: (0, 0)),
                  pl.BlockSpec((1, d), lambda i: (0, 0))],
        out_specs=pl.BlockSpec((tm, d), lambda i: (i, 0)),
        out_shape=jax.ShapeDtypeStruct((n, d), F32),
        compiler_params=_cparams(("parallel",)),
        name="proj_ln",
    )(a, w, res, gain.reshape(1, d), bias.reshape(1, d))


def _t5_bucket(dist):
    n = jnp.maximum(dist, 0)
    nf = jnp.maximum(n, 1).astype(F32)
    large = REL_EXACT + (jnp.log(nf / REL_EXACT) / math.log(REL_MAX_DIST / REL_EXACT)
                         * (REL_BUCKETS - REL_EXACT)).astype(jnp.int32)
    return jnp.where(n < REL_EXACT, n, jnp.minimum(large, REL_BUCKETS - 1))


def _bias_lookup(bucket, tab_ref, col):
    out = jnp.zeros(bucket.shape, F32)
    for k in range(REL_BUCKETS):
        out = jnp.where(bucket == k, tab_ref[k, col], out)
    return out


def _per_head_scalar(n_heads, fn):
    hidx = lax.broadcasted_iota(jnp.int32, (n_heads, 1, 1), 0)
    out = jnp.zeros((n_heads, 1, 1), F32)
    for h in range(n_heads):
        out = jnp.where(hidx == h, fn(h), out)
    return out


def _split3_const(value):
    x = np.float32(value)
    hi = np.float32(np.asarray(x, dtype=jnp.bfloat16))
    mid = np.float32(np.asarray(x - hi, dtype=jnp.bfloat16))
    lo = np.float32(np.asarray(x - hi - mid, dtype=jnp.bfloat16))
    return hi, mid, lo


def _split3(x):
    hi = x.astype(BF16)
    r1 = x - hi.astype(F32)
    mid = r1.astype(BF16)
    lo = (r1 - mid.astype(F32)).astype(BF16)
    return hi, mid, lo


def _nsa_compress_kernel(zk_ref, zv_ref, pos_ref, w1_ref, w2_ref, w2t_ref, kc_ref, vct_ref):
    def hidden(idx, z_ref):
        z = z_ref[0, 0]
        u = _dot((z + pos_ref[idx, 0]).astype(BF16), w1_ref[idx, 0])
        v = _dot((z + pos_ref[idx, 1]).astype(BF16), w1_ref[idx, 1])
        return jax.nn.gelu(u + pltpu.roll(v, z.shape[0] - 1, 0)).astype(BF16)

    kc_ref[0, 0] = _dot(hidden(0, zk_ref), w2_ref[0]).astype(kc_ref.dtype)
    vct_ref[0, 0] = _dot_nt(w2t_ref[1], hidden(1, zv_ref)).astype(vct_ref.dtype)


def _nsa_compress(zk, zv, pos, w1, w2):
    b, g, nc, kd = zk.shape
    hid = w1.shape[-1]
    dh = w2.shape[-1]
    zspec = pl.BlockSpec((1, 1, nc, kd), lambda i, j: (i, j, 0, 0))
    return pl.pallas_call(
        _nsa_compress_kernel,
        grid=(b, g),
        in_specs=[zspec, zspec,
                  pl.BlockSpec((2, 2, 1, kd), lambda i, j: (0, 0, 0, 0)),
                  pl.BlockSpec((2, 2, kd, hid), lambda i, j: (0, 0, 0, 0)),
                  pl.BlockSpec((2, hid, dh), lambda i, j: (0, 0, 0)),
                  pl.BlockSpec((2, dh, hid), lambda i, j: (0, 0, 0))],
        out_specs=[pl.BlockSpec((1, 1, nc, dh), lambda i, j: (i, j, 0, 0)),
                   pl.BlockSpec((1, 1, dh, nc), lambda i, j: (i, j, 0, 0))],
        out_shape=[jax.ShapeDtypeStruct((b, g, nc, dh), BF16), jax.ShapeDtypeStruct((b, g, dh, nc), BF16)],
        compiler_params=_cparams(("parallel", "parallel")),
        name="nsa_compress",
    )(zk, zv, pos, w1, w2, w2.transpose(0, 2, 1))


def _nsa_attn_kernel(tab_ref, qt_ref, kc_ref, vct_ref, ks_ref, vst_ref, kw_ref, vwt_ref, gate_ref, o_ref,
                     bn_ref, edge_ref, cb_ref, vt_ref, neg_ref, add_ref, m_ref, acc_ref, mw_ref, accw_ref,
                     s0_ref, s1_ref, m0_ref, m1_ref, *, n_sel):
    tq = NSA_TQ
    tkf = NSA_TKF
    nq = NSA_QPG
    dh = HEAD_DIM
    w = nq * tq
    g = pl.program_id(1)
    iq = pl.program_id(2)
    t0 = iq * tq
    n_cmp = kc_ref.shape[2]
    n_blk = vt_ref.shape[0]
    bpt = tq // SEL_BLOCK
    far_bucket = REL_BUCKETS - 1

    def head_cols(fn):
        return jnp.concatenate([fn(h) for h in range(nq)], axis=1)

    far_row = head_cols(lambda h: jnp.full((1, tq), tab_ref[far_bucket, g * nq + h], F32))

    @pl.when(iq == 0)
    def _init():
        jj = lax.broadcasted_iota(jnp.int32, (tq, tq), 0)
        ii = lax.broadcasted_iota(jnp.int32, (tq, tq), 1)
        d = ii - jj
        b_diag = _t5_bucket(d)
        b_prev = _t5_bucket(d + tq)
        mm = lax.broadcasted_iota(jnp.int32, (NSA_CB_ROWS, tq), 0)
        i2 = lax.broadcasted_iota(jnp.int32, (NSA_CB_ROWS, tq), 1)
        dn = i2 - CMP_STRIDE * mm + (9 * CMP_STRIDE - CMP_LEN + 1)
        b_near = _t5_bucket(dn)
        for h in range(nq):
            col = g * nq + h
            far = tab_ref[far_bucket, col]
            cols = slice(h * tq, (h + 1) * tq)
            bn_ref[0:tq, cols] = _bias_lookup(b_prev, tab_ref, col)
            bn_ref[tq:2 * tq, cols] = jnp.where(d >= 0, _bias_lookup(b_diag, tab_ref, col), NEG_INF)
            edge_ref[:, cols] = jnp.where(jj > ii, far, NEG_INF)
            near = jnp.where(dn >= 0, _bias_lookup(b_near, tab_ref, col), NEG_INF)
            cbv = jnp.where(mm < NSA_NEAR, near,
                            jnp.where(mm == NSA_NEAR, far, jnp.where(mm == NSA_NEAR + 1, NEG_INF, 0.0)))
            hi, mid, lo = _split3(cbv)
            cb_ref[0, :, cols] = hi
            cb_ref[1, :, cols] = mid
            cb_ref[2, :, cols] = lo

    qt = (head_cols(lambda h: qt_ref[0, h]).astype(F32) * (dh ** -0.5)).astype(BF16)

    neg_parts = [float(v) for v in _split3_const(NEG_INF)]
    spare = lax.broadcasted_iota(jnp.int32, (16, 1), 0)
    pad_rows = (jnp.where(spare == 0, neg_parts[0], jnp.where(spare == 1, neg_parts[1],
                                                             jnp.where(spare == 2, neg_parts[2], 0.0)))
                + jnp.zeros((16, w), F32)).astype(BF16)

    def aug_rhs(pieces):
        return jnp.concatenate([qt] + list(pieces) + [pad_rows], axis=0)

    zero16 = jnp.zeros((16, w), BF16)
    rhs_plain = aug_rhs([zero16, zero16, zero16])

    def kv_start(key0):
        return pl.multiple_of(key0 + tq, LANE)

    def flash_first(m_st, acc_st, s, vt):
        m = jnp.max(s, axis=0, keepdims=True)
        m_st[...] = m
        acc_st[...] = _dot(vt, jnp.exp(s - m).astype(BF16))

    def flash_next(m_st, acc_st, s, s_max, vt):
        m_old = m_st[...]
        m_new = jnp.maximum(m_old, s_max)
        acc_st[...] = jnp.exp(m_old - m_new) * acc_st[...] + _dot(vt, jnp.exp(s - m_new).astype(BF16))
        m_st[...] = m_new

    def flash_out(acc_st):
        acc = acc_st[...]
        return acc[0:dh] * (1.0 / acc[dh:dh + 1])

    n_lo = t0 // CMP_STRIDE - 9
    ni = lax.broadcasted_iota(jnp.int32, (n_cmp, NSA_CB_ROWS), 0)
    mi = lax.broadcasted_iota(jnp.int32, (n_cmp, NSA_CB_ROWS), 1)
    place = (((mi < NSA_NEAR) & (ni == mi + n_lo)) | ((mi == NSA_NEAR) & (ni < n_lo))
             | ((mi == NSA_NEAR + 1) & (ni >= n_lo + NSA_NEAR)))
    shift = jnp.where(place, 1.0, 0.0).astype(BF16)
    lhs_c = jnp.concatenate([kc_ref[0, 0], shift, shift, shift], axis=1)
    rhs_c = jnp.concatenate([qt, cb_ref[0], cb_ref[1], cb_ref[2]], axis=0)
    lcm = _dot(lhs_c, rhs_c)
    ec = jnp.exp(lcm - jnp.max(lcm, axis=0, keepdims=True))
    sc = jnp.sum(ec, axis=0, keepdims=True)
    tl = t0 + lax.broadcasted_iota(jnp.int32, (1, w), 1) % tq
    pct = ec * jnp.where(tl >= CMP_LEN - 1, 1.0 / sc, 0.0)
    o_ct = _dot(vct_ref[0, 0], pct.astype(BF16))

    psum = pct[:, 0:tq]
    for h in range(1, nq):
        psum = psum + pct[:, h * tq:(h + 1) * tq]
    p_hi = psum.astype(BF16)
    p_lo = (psum - p_hi.astype(F32)).astype(BF16)
    jb = lax.broadcasted_iota(jnp.int32, (n_blk, n_cmp), 0)
    nb = lax.broadcasted_iota(jnp.int32, (n_blk, n_cmp), 1)
    ovl = jnp.where((CMP_STRIDE * nb < SEL_BLOCK * (jb + 1)) & (CMP_STRIDE * nb + CMP_LEN > SEL_BLOCK * jb),
                    1.0, 0.0).astype(BF16)
    imp_t = _dot(ovl, p_hi) + _dot(ovl, p_lo)
    jb2 = lax.broadcasted_iota(jnp.int32, (n_blk, tq), 0)
    cur = (t0 + lax.broadcasted_iota(jnp.int32, (n_blk, tq), 1)) // SEL_BLOCK
    forced = (jb2 == 0) | (jb2 == cur) | (jb2 == cur - 1)
    val = jnp.where(jb2 > cur, -1e9, jnp.where(forced, 1e9, imp_t))
    vt_ref[...] = val

    def rank_body(jp, rank):
        row = vt_ref[pl.ds(jp, 1), :]
        beats = (row > val) | ((row == val) & (jp < jb2))
        return rank + jnp.where(beats, 1, 0)

    n_causal = (t0 + tq) // SEL_BLOCK
    rank = lax.fori_loop(0, jnp.where(n_causal <= n_sel, 0, n_causal), rank_body,
                         jnp.zeros((n_blk, tq), jnp.int32))
    neg_t = jnp.where(rank < n_sel, 0.0, NEG_INF)
    neg4 = jnp.concatenate([neg_t] * nq, axis=1)
    neg_ref[...] = neg4

    blk0 = iq * bpt
    n_far = jnp.maximum(blk0 - bpt, 0)

    def split_rows(masked, live_row):
        return [jnp.where(masked, neg_parts[i], piece.astype(F32)) for i, piece in enumerate(_split3(live_row))]

    brow = lax.broadcasted_iota(jnp.int32, (n_blk, 1), 0)
    far_parts = split_rows((neg4 < 0.0) | (brow >= n_far), far_row)
    for part in range(3):
        add_ref[part] = far_parts[part].reshape(n_blk // 16, 16, w).astype(BF16)

    near_rows = []
    for j in range(16):
        off = (j - (blk0 - bpt)) % 16
        row = neg_ref[pl.ds(jnp.clip(blk0 - bpt + off, 0, n_blk - 1), 1), :]
        near_rows.append(jnp.where(off < 2 * bpt, row, 0.0))
    near_mask = jnp.concatenate(near_rows, axis=0) < 0.0
    rhs_near = aug_rhs([p.astype(BF16) for p in split_rows(near_mask, jnp.zeros((1, w), F32))])

    near0 = kv_start(t0 - tq)
    edge0 = kv_start(jnp.maximum(t0 - 2 * tq, -tq))
    s_win = _dot(kw_ref[0, 0, pl.ds(near0, 2 * tq), :], rhs_plain) + bn_ref[...]
    s_edge = _dot(kw_ref[0, 0, pl.ds(edge0, tq), :], rhs_plain) + edge_ref[...]
    flash_first(mw_ref, accw_ref, s_win, vwt_ref[0, 0, :, pl.ds(near0, 2 * tq)])
    flash_next(mw_ref, accw_ref, s_edge, jnp.max(s_edge, axis=0, keepdims=True), vwt_ref[0, 0, :, pl.ds(edge0, tq)])
    o_wt = flash_out(accw_ref)

    s_sel = _dot(ks_ref[0, 0, pl.ds(near0, 2 * tq), :], rhs_near) + bn_ref[...]
    flash_first(m_ref, acc_ref, s_sel, vst_ref[0, 0, :, pl.ds(near0, 2 * tq)])

    bpf = tkf // SEL_BLOCK
    n_tiles = n_far // bpf

    def qk_stage(c, s_buf, m_buf):
        chunk = (c * bpf) // 16
        s = _dot(ks_ref[0, 0, pl.ds(kv_start(c * tkf), tkf), :],
                 aug_rhs([add_ref[0, chunk], add_ref[1, chunk], add_ref[2, chunk]]))
        s_buf[...] = s
        m_buf[...] = jnp.max(s, axis=0, keepdims=True)

    def sm_stage(c, s_buf, m_buf):
        flash_next(m_ref, acc_ref, s_buf[...], m_buf[...], vst_ref[0, 0, :, pl.ds(kv_start(c * tkf), tkf)])

    @pl.when(n_tiles > 0)
    def _():
        qk_stage(0, s0_ref, m0_ref)

    def far_body(i, carry):
        qk_stage(2 * i + 1, s1_ref, m1_ref)
        sm_stage(2 * i, s0_ref, m0_ref)
        qk_stage(2 * i + 2, s0_ref, m0_ref)
        sm_stage(2 * i + 1, s1_ref, m1_ref)
        return carry

    lax.fori_loop(0, (n_tiles + 1) // 2, far_body, 0)
    o_st = flash_out(acc_ref)

    sg = jax.nn.sigmoid(gate_ref[0, 0])
    outs = []
    for h in range(nq):
        cols = slice(h * tq, (h + 1) * tq)
        outs.append(sg[3 * h:3 * h + 1] * o_ct[:, cols] + sg[3 * h + 1:3 * h + 2] * o_st[:, cols]
                    + sg[3 * h + 2:3 * h + 3] * o_wt[:, cols])
    o_ref[0] = jnp.concatenate(outs, axis=0).T.astype(o_ref.dtype)


def _nsa_attention(tab, qt, kc, vct, ks, vst, kw, vwt, gate):
    b, _, dh, t = qt.shape
    g = NSA_KV_HEADS
    nq = NSA_QPG
    tq = NSA_TQ
    w = nq * tq
    assert t % tq == 0 and tq >= REL_MAX_DIST and NSA_WINDOW == 2 * tq and tq % SEL_BLOCK == 0
    assert NSA_TKF % SEL_BLOCK == 0 and t >= NSA_TKF and tq % LANE == 0
    assert NSA_NEAR + 1 < NSA_CB_ROWS
    n_cmp = kc.shape[2]
    n_blk = t // SEL_BLOCK
    n_sel = min(SEL_COUNT, n_blk)
    assert n_sel >= 3
    tp = ks.shape[2]
    bpf = NSA_TKF // SEL_BLOCK
    assert tp == t + tq and vst.shape[3] == tp and 16 % bpf == 0 and n_blk % 16 == 0 and tq % NSA_TKF == 0
    assert ks.shape[3] == 2 * dh and vst.shape[2] == dh + 16
    kspec = pl.BlockSpec((1, 1, tp, 2 * dh), lambda i, j, k: (i, j, 0, 0))
    vtspec = pl.BlockSpec((1, 1, dh + 16, tp), lambda i, j, k: (i, j, 0, 0))
    return pl.pallas_call(
        functools.partial(_nsa_attn_kernel, n_sel=n_sel),
        grid=(b, g, t // tq),
        in_specs=[pl.BlockSpec(memory_space=pltpu.SMEM),
                  pl.BlockSpec((1, nq, dh, tq), lambda i, j, k: (i, j, 0, k)),
                  pl.BlockSpec((1, 1, n_cmp, dh), lambda i, j, k: (i, j, 0, 0)),
                  pl.BlockSpec((1, 1, dh, n_cmp), lambda i, j, k: (i, j, 0, 0)),
                  kspec, vtspec, kspec, vtspec,
                  pl.BlockSpec((1, 1, 16, tq), lambda i, j, k: (i, j, 0, k))],
        out_specs=pl.BlockSpec((1, tq, nq * dh), lambda i, j, k: (i, k, j)),
        out_shape=jax.ShapeDtypeStruct((b, t, g * nq * dh), BF16),
        scratch_shapes=[pltpu.VMEM((2 * tq, w), F32),
                        pltpu.VMEM((tq, w), F32),
                        pltpu.VMEM((3, NSA_CB_ROWS, w), BF16),
                        pltpu.VMEM((n_blk, tq), F32),
                        pltpu.VMEM((n_blk, w), F32),
                        pltpu.VMEM((3, n_blk // 16, 16, w), BF16),
                        pltpu.VMEM((1, w), F32),
                        pltpu.VMEM((dh + 16, w), F32),
                        pltpu.VMEM((1, w), F32),
                        pltpu.VMEM((dh + 16, w), F32),
                        pltpu.VMEM((NSA_TKF, w), F32), pltpu.VMEM((NSA_TKF, w), F32),
                        pltpu.VMEM((1, w), F32), pltpu.VMEM((1, w), F32)],
        compiler_params=_cparams(("parallel", "parallel", "arbitrary")),
        name="nsa_attn",
    )(tab, qt, kc, vct, ks, vst, kw, vwt, gate)


def _nsa_layer(x, w_in, w_out, cmp_pos, cmp_w1, cmp_w2, rel_bias, gain, bias, alpha):
    b, t, d = x.shape
    g, nq, dh = NSA_KV_HEADS, NSA_QPG, HEAD_DIM
    kvw = g * dh
    n_in = w_in.shape[1]
    n_pad = -n_in % (3 * LANE)
    w_in_p = jnp.pad(w_in, ((0, 0), (0, n_pad))).astype(BF16)
    x2 = x.reshape(b * t, d)
    h = _proj(x2, w_in_p, tn=3 * LANE).reshape(b, t, n_in + n_pad)

    def heads(lo, n_heads, dt, perm=(0, 2, 1, 3)):
        z = h[:, :, lo:lo + n_heads * dh].reshape(b, t, n_heads, dh)
        return z.transpose(perm).astype(dt)

    head_t = (0, 2, 3, 1)
    qt = heads(0, ATTN_HEADS, BF16, head_t)
    off = ATTN_HEADS * dh
    zk = heads(off, g, F32).reshape(b, g, t // CMP_STRIDE, CMP_STRIDE * dh)
    zv = heads(off + kvw, g, F32).reshape(b, g, t // CMP_STRIDE, CMP_STRIDE * dh)
    tp = t + NSA_TQ
    is_pad = np.arange(tp) < NSA_TQ
    slot = (np.arange(tp) - NSA_TQ) // SEL_BLOCK % 16
    onehot = ((np.arange(16)[None, :] == slot[:, None]) & ~is_pad[:, None]).astype(np.float32)
    pad_cols = ((np.arange(16)[None, :] < 3) & is_pad[:, None]).astype(np.float32)
    k_cols = jnp.asarray(np.concatenate([onehot] * 3 + [pad_cols], axis=1), BF16)
    v_rows = jnp.asarray((np.arange(16)[:, None] == 0) * np.ones((1, tp)), BF16)

    def aug_k(z):
        z = jnp.pad(z, ((0, 0), (0, 0), (NSA_TQ, 0), (0, 0)))
        return jnp.concatenate([z, jnp.broadcast_to(k_cols, (b, g, tp, dh))], axis=3)

    def aug_vt(z):
        z = jnp.pad(z, ((0, 0), (0, 0), (0, 0), (NSA_TQ, 0)))
        return jnp.concatenate([z, jnp.broadcast_to(v_rows, (b, g, 16, tp))], axis=2)

    ks = aug_k(heads(off + 2 * kvw, g, BF16))
    vst = aug_vt(heads(off + 3 * kvw, g, BF16, head_t))
    kw = aug_k(heads(off + 4 * kvw, g, BF16))
    vwt = aug_vt(heads(off + 5 * kvw, g, BF16, head_t))
    gate = h[:, :, off + 6 * kvw:off + 6 * kvw + 3 * ATTN_HEADS].reshape(b, t, g, 3 * nq)
    gate = jnp.pad(gate.transpose(0, 2, 3, 1), ((0, 0), (0, 0), (0, 16 - 3 * nq), (0, 0)))

    pos = cmp_pos.reshape(2, 2, 1, CMP_STRIDE * dh)
    w1 = cmp_w1.reshape(2, 2, CMP_STRIDE * dh, cmp_w1.shape[-1]).astype(BF16)
    kc, vct = _nsa_compress(zk, zv, pos, w1, cmp_w2.astype(BF16))
    o = _nsa_attention(rel_bias, qt, kc, vct, ks, vst, kw, vwt, gate)
    y = _proj_ln(o.reshape(b * t, d), w_out.astype(BF16), x2, gain, bias, alpha)
    return y.reshape(b, t, d)


def _hgrn_kernel(lbp_ref, gain_ref, zq_ref, zf_ref, zi_ref, zg_ref, o_ref, st_ref, *, layer):
    C = HGRN_TC
    t = zq_ref.shape[1]
    kdim = zq_ref.shape[2]

    p = lbp_ref[...]
    e = jnp.exp(p - jnp.max(p, axis=0, keepdims=True))
    sm = e / jnp.sum(e, axis=0, keepdims=True)
    cs = sm[0:1]
    for r in range(1, layer + 1):
        cs = cs + sm[r:r + 1]
    lb = cs - sm[0:1]
    log_lb = jnp.log(lb)
    log_1m = jnp.log1p(-lb)
    gain = gain_ref[...]

    st_ref[...] = jnp.zeros(st_ref.shape, F32)

    rr = lax.broadcasted_iota(jnp.int32, (C, C), 0)
    cc = lax.broadcasted_iota(jnp.int32, (C, C), 1)
    srow = lax.broadcasted_iota(jnp.int32, (C, 1), 0)
    halves = [1 << i for i in range(C.bit_length() - 1)]
    small = [h for h in halves if h < 8]
    sel_rows = [jnp.where(cc <= rr, 1.0, 0.0).astype(BF16)]
    sel_rows += [jnp.where(cc <= (rr // (2 * h)) * (2 * h) + h - 1, 1.0, 0.0).astype(BF16) for h in small]
    cum_sel = jnp.concatenate(sel_rows, axis=0)
    right = [(srow // h) % 2 == 1 for h in halves]
    same_blk = [rr // (2 * h) == cc // (2 * h) for h in halves]
    eye = rr == cc

    def chunk(c, carry):
        r0 = pl.multiple_of(c * C, C)
        zq = zq_ref[0, pl.ds(r0, C), :]
        zf = zf_ref[0, pl.ds(r0, C), :]
        v = zi_ref[0, pl.ds(r0, C), :]
        zg = zg_ref[0, pl.ds(r0, C), :]
        q = jax.nn.silu(zq)
        log_f = jnp.logaddexp(log_lb, log_1m + jax.nn.log_sigmoid(zf))
        kk = (1.0 - lb) * jax.nn.sigmoid(-zf)
        vb = v.astype(BF16)

        g_hi, g_mid, g_lo = _split3(log_f)
        cums = _dot(cum_sel, g_hi) + _dot(cum_sel, g_mid) + _dot(cum_sel, g_lo)
        bcum = cums[0:C]
        b_last = bcum[C - 1:C, :]

        st = st_ref[...]
        o = _dot_nt((q * jnp.exp(bcum)).astype(BF16), st.astype(BF16))

        a = jnp.where(eye, _dot_nt(q.astype(BF16), kk.astype(BF16)), 0.0)
        for lvl, h in enumerate(halves):
            if h in small:
                b_ref = cums[(lvl + 1) * C:(lvl + 2) * C]
            else:
                b_ref = jnp.broadcast_to(bcum.reshape(C // (2 * h), 2 * h, kdim)[:, h - 1:h, :],
                                         (C // (2 * h), 2 * h, kdim)).reshape(C, kdim)
            e = jnp.exp(-jnp.abs(bcum - b_ref))
            q_side = jnp.where(right[lvl], q * e, 0.0)
            k_side = jnp.where(right[lvl], 0.0, kk * e)
            a = a + jnp.where(same_blk[lvl], _dot_nt(q_side.astype(BF16), k_side.astype(BF16)), 0.0)
        o = o + _dot(a.astype(BF16), vb)

        kd = kk * jnp.exp(b_last - bcum)
        st_ref[...] = st * jnp.exp(b_last) + _dot(v.T.astype(BF16), kd.astype(BF16))

        o = o * lax.rsqrt(jnp.mean(o * o, axis=-1, keepdims=True) + RMS_EPS) * gain
        o = o * jax.nn.silu(zg)
        o_ref[0, pl.ds(r0, C), :] = o.astype(o_ref.dtype)
        return carry

    lax.fori_loop(0, t // C, chunk, 0, unroll=4)


def _hgrn_layer(x, w_in, w_out, norm_gain, lb_param, layer, gain, bias, alpha):
    b, t, d = x.shape
    nh = HGRN_HEADS
    kd = d // nh
    x2 = x.reshape(b * t, d)
    h = _proj(x2, w_in.astype(BF16), tn=1024).reshape(b, t, 4 * d)

    def zspec(part):
        return pl.BlockSpec((1, t, kd), lambda i, j: (i, 0, part * nh + j))

    o = pl.pallas_call(
        functools.partial(_hgrn_kernel, layer=layer),
        grid=(b, nh),
        in_specs=[pl.BlockSpec((lb_param.shape[0], kd), lambda i, j: (0, j)),
                  pl.BlockSpec((1, kd), lambda i, j: (0, 0)),
                  zspec(0), zspec(1), zspec(2), zspec(3)],
        out_specs=pl.BlockSpec((1, t, kd), lambda i, j: (i, 0, j)),
        out_shape=jax.ShapeDtypeStruct((b, t, d), BF16),
        scratch_shapes=[pltpu.VMEM((kd, kd), F32)],
        compiler_params=_cparams(("parallel", "parallel")),
        name="hgrn",
    )(lb_param, norm_gain.reshape(1, kd), h, h, h, h)
    y = _proj_ln(o.reshape(b * t, d), w_out.astype(BF16), x2, gain, bias, alpha)
    return y.reshape(b, t, d)


def _swa_kernel(tab_ref, sink_ref, q_ref, k_ref, v_ref, o_ref, bd_ref, bp_ref):
    L = SWA_WINDOW
    nq = SWA_QPG
    dh = HEAD_DIM
    g = pl.program_id(1)
    n = pl.program_id(2)
    ii = lax.broadcasted_iota(jnp.int32, (L, L), 0)
    jj = lax.broadcasted_iota(jnp.int32, (L, L), 1)

    @pl.when(n == 0)
    def _init():
        b_diag = _t5_bucket(ii - jj)
        b_prev = _t5_bucket(ii - jj + L)
        for h in range(nq):
            bd_ref[h] = _bias_lookup(b_diag, tab_ref, g * nq + h)
            bp_ref[h] = _bias_lookup(b_prev, tab_ref, g * nq + h)

    qs = (q_ref[0].astype(F32) * (dh ** -0.5)).astype(BF16).reshape(nq * L, dh)
    cur = pl.multiple_of(n * L, L)
    prev = pl.multiple_of(jnp.maximum(n - 1, 0) * L, L)
    k_c = k_ref[0, 0, pl.ds(cur, L), :]
    v_c = v_ref[0, 0, pl.ds(cur, L), :]
    k_p = k_ref[0, 0, pl.ds(prev, L), :]
    v_p = v_ref[0, 0, pl.ds(prev, L), :]

    s_c = _dot_nt(qs, k_c).reshape(nq, L, L) + bd_ref[...]
    s_p = _dot_nt(qs, k_p).reshape(nq, L, L) + bp_ref[...]
    s_c = jnp.where((ii >= jj)[None], s_c, NEG_INF)
    s_p = jnp.where(((jj > ii) & (n > 0))[None], s_p, NEG_INF)
    sink = _per_head_scalar(nq, lambda h: sink_ref[g * nq + h])
    m = jnp.maximum(jnp.maximum(jnp.max(s_c, axis=-1, keepdims=True), jnp.max(s_p, axis=-1, keepdims=True)), sink)
    e_c = jnp.exp(s_c - m)
    e_p = jnp.exp(s_p - m)
    den = jnp.sum(e_c, axis=-1, keepdims=True) + jnp.sum(e_p, axis=-1, keepdims=True) + jnp.exp(sink - m)
    pv = (_dot(e_c.reshape(nq * L, L).astype(BF16), v_c) + _dot(e_p.reshape(nq * L, L).astype(BF16), v_p))
    o = pv.reshape(nq, L, dh) / den
    o_ref[0] = jnp.concatenate([o[h] for h in range(nq)], axis=1).astype(o_ref.dtype)


def _swa_layer(x, w_in, w_out, sinks, rel_bias, gain, bias, alpha):
    b, t, d = x.shape
    kvh, nq, dh, L = SWA_KV_HEADS, SWA_QPG, HEAD_DIM, SWA_WINDOW
    assert t % L == 0 and L >= REL_MAX_DIST
    x2 = x.reshape(b * t, d)
    n_in = w_in.shape[1]
    h = _proj(x2, w_in.astype(BF16), tn=n_in // 2).reshape(b, t, n_in)

    def heads(lo, n_heads):
        z = h[:, :, lo:lo + n_heads * dh].reshape(b, t, n_heads, dh)
        return z.transpose(0, 2, 1, 3).astype(BF16)

    q = heads(0, ATTN_HEADS)
    k = heads(ATTN_HEADS * dh, kvh)
    v = heads(ATTN_HEADS * dh + kvh * dh, kvh)
    kvspec = pl.BlockSpec((1, 1, t, dh), lambda i, j, n: (i, j, 0, 0))
    o = pl.pallas_call(
        _swa_kernel,
        grid=(b, kvh, t // L),
        in_specs=[pl.BlockSpec(memory_space=pltpu.SMEM),
                  pl.BlockSpec(memory_space=pltpu.SMEM),
                  pl.BlockSpec((1, nq, L, dh), lambda i, j, n: (i, j, n, 0)),
                  kvspec, kvspec],
        out_specs=pl.BlockSpec((1, L, nq * dh), lambda i, j, n: (i, n, j)),
        out_shape=jax.ShapeDtypeStruct((b, t, d), BF16),
        scratch_shapes=[pltpu.VMEM((nq, L, L), F32), pltpu.VMEM((nq, L, L), F32)],
        compiler_params=_cparams(("parallel", "parallel", "arbitrary")),
        name="swa_attn",
    )(rel_bias, sinks, q, k, v)
    y = _proj_ln(o.reshape(b * t, d), w_out.astype(BF16), x2, gain, bias, alpha)
    return y.reshape(b, t, d)


def kernel(x, rel_bias, ln_gain, ln_bias, ffn1_w_gate, ffn1_w_up, ffn1_w_down, ffn2_w_gate, ffn2_w_up,
           ffn2_w_down, nsa_w_in, nsa_w_out, nsa_cmp_pos, nsa_cmp_w1, nsa_cmp_w2, hgrn_w_in, hgrn_w_out,
           hgrn_norm_gain, hgrn_lb, swa_w_in, swa_w_out, swa_sinks):
    depth = ln_gain.shape[0]
    alpha = (2.0 * depth) ** 0.25
    b, t, d = x.shape

    def ffn(x, wg, wu, wd, gain, bias):
        y = _ffn_ln(x.reshape(b * t, d), wg.astype(BF16), wu.astype(BF16), wd.astype(BF16), gain, bias, alpha)
        return y.reshape(b, t, d)

    for i in range(depth):
        x = ffn(x, ffn1_w_gate[i], ffn1_w_up[i], ffn1_w_down[i], ln_gain[i, 0], ln_bias[i, 0])
        kind, slot = i % N_MIXERS, i // N_MIXERS
        if kind == 0:
            x = _nsa_layer(x, nsa_w_in[slot], nsa_w_out[slot], nsa_cmp_pos[slot], nsa_cmp_w1[slot],
                           nsa_cmp_w2[slot], rel_bias, ln_gain[i, 1], ln_bias[i, 1], alpha)
        elif kind == 1:
            x = _hgrn_layer(x, hgrn_w_in[slot], hgrn_w_out[slot], hgrn_norm_gain[slot], hgrn_lb, i,
                            ln_gain[i, 1], ln_bias[i, 1], alpha)
        else:
            x = _swa_layer(x, swa_w_in[slot], swa_w_out[slot], swa_sinks[slot], rel_bias,
                           ln_gain[i, 1], ln_bias[i, 1], alpha)
        x = ffn(x, ffn2_w_gate[i], ffn2_w_up[i], ffn2_w_down[i], ln_gain[i, 2], ln_bias[i, 2])
    return x
```

```python
import functools
import math

import jax
import jax.numpy as jnp
import numpy as np
from jax import lax
from jax.experimental import pallas as pl
from jax.experimental.pallas import tpu as pltpu

F32 = jnp.float32
BF16 = jnp.bfloat16

DEPTH = 4
N_MIXERS = 3
REL_BUCKETS = 32
REL_EXACT = REL_BUCKETS // 2
REL_MAX_DIST = 128
ATTN_HEADS = 16
HEAD_DIM = 64
NSA_KV_HEADS = 4
NSA_QPG = ATTN_HEADS // NSA_KV_HEADS
CMP_STRIDE = 16
CMP_LEN = 2 * CMP_STRIDE
SEL_BLOCK = 64
SEL_COUNT = 16
NSA_WINDOW = 512
HGRN_HEADS = 8
HGRN_CHUNK = 64
HGRN_TC = 128
SWA_KV_HEADS = 2
SWA_QPG = ATTN_HEADS // SWA_KV_HEADS
SWA_WINDOW = 128
LN_EPS = 1e-5
RMS_EPS = 1e-6
NEG_INF = -1e30

LANE = 128
VMEM_LIMIT = 48 * 1024 * 1024
NSA_TQ = 256
NSA_TKF = 256
NSA_NEAR = NSA_TQ // CMP_STRIDE + 8
NSA_CB_ROWS = 32


def _cparams(sem):
    return pltpu.CompilerParams(dimension_semantics=sem, vmem_limit_bytes=VMEM_LIMIT)


def _dot(a, b):
    return jnp.dot(a, b, preferred_element_type=F32)


def _dot_nt(a, b):
    return lax.dot_general(a, b, (((1,), (1,)), ((), ())), preferred_element_type=F32)


def _layer_norm_rows(y, g, b):
    mu = jnp.mean(y, axis=-1, keepdims=True)
    yc = y - mu
    var = jnp.mean(yc * yc, axis=-1, keepdims=True)
    return yc * lax.rsqrt(var + LN_EPS) * g + b


def _ffn_ln_kernel(x_ref, wg_ref, wu_ref, wd_ref, g_ref, b_ref, o_ref, xb_ref, acc_ref, *, alpha):
    j = pl.program_id(1)

    @pl.when(j == 0)
    def _():
        xb_ref[...] = x_ref[...].astype(BF16)
        acc_ref[...] = jnp.zeros_like(acc_ref)

    xb = xb_ref[...]
    gate = _dot(xb, wg_ref[...])
    up = _dot(xb, wu_ref[...])
    h = (jax.nn.silu(gate) * up).astype(BF16)
    acc_ref[...] += _dot(h, wd_ref[...])

    @pl.when(j == pl.num_programs(1) - 1)
    def _():
        y = alpha * x_ref[...] + 0.5 * acc_ref[...]
        o_ref[...] = _layer_norm_rows(y, g_ref[...], b_ref[...])


def _ffn_ln(x, wg, wu, wd, gain, bias, alpha, tm=512, tf=1408):
    n, d = x.shape
    f = wg.shape[1]
    tm = min(tm, n)
    assert n % tm == 0 and f % tf == 0
    return pl.pallas_call(
        functools.partial(_ffn_ln_kernel, alpha=alpha),
        grid=(n // tm, f // tf),
        in_specs=[
            pl.BlockSpec((tm, d), lambda i, j: (i, 0)),
            pl.BlockSpec((d, tf), lambda i, j: (0, j)),
            pl.BlockSpec((d, tf), lambda i, j: (0, j)),
            pl.BlockSpec((tf, d), lambda i, j: (j, 0)),
            pl.BlockSpec((1, d), lambda i, j: (0, 0)),
            pl.BlockSpec((1, d), lambda i, j: (0, 0)),
        ],
        out_specs=pl.BlockSpec((tm, d), lambda i, j: (i, 0)),
        out_shape=jax.ShapeDtypeStruct((n, d), F32),
        scratch_shapes=[pltpu.VMEM((tm, d), BF16), pltpu.VMEM((tm, d), F32)],
        compiler_params=_cparams(("parallel", "arbitrary")),
        name="ffn_ln",
    )(x, wg, wu, wd, gain.reshape(1, d), bias.reshape(1, d))


def _proj_kernel(x_ref, w_ref, o_ref, xb_ref):
    @pl.when(pl.program_id(1) == 0)
    def _():
        xb_ref[...] = x_ref[...].astype(BF16)

    o_ref[...] = _dot(xb_ref[...], w_ref[...]).astype(o_ref.dtype)


def _proj(x, w, tn, tm=1024, out_dtype=F32):
    n, k = x.shape
    m = w.shape[1]
    tm = min(tm, n)
    assert n % tm == 0 and m % tn == 0
    return pl.pallas_call(
        _proj_kernel,
        grid=(n // tm, m // tn),
        in_specs=[pl.BlockSpec((tm, k), lambda i, j: (i, 0)),
                  pl.BlockSpec((k, tn), lambda i, j: (0, j))],
        out_specs=pl.BlockSpec((tm, tn), lambda i, j: (i, j)),
        out_shape=jax.ShapeDtypeStruct((n, m), out_dtype),
        scratch_shapes=[pltpu.VMEM((tm, k), BF16)],
        compiler_params=_cparams(("parallel", "arbitrary")),
        name="proj",
    )(x, w)


def _proj_ln_kernel(a_ref, w_ref, r_ref, g_ref, b_ref, o_ref, *, alpha):
    y = _dot(a_ref[...].astype(BF16), w_ref[...])
    o_ref[...] = _layer_norm_rows(alpha * r_ref[...] + y, g_ref[...], b_ref[...])


def _proj_ln(a, w, res, gain, bias, alpha, tm=512):
    n, k = a.shape
    d = w.shape[1]
    tm = min(tm, n)
    assert n % tm == 0
    return pl.pallas_call(
        functools.partial(_proj_ln_kernel, alpha=alpha),
        grid=(n // tm,),
        in_specs=[pl.BlockSpec((tm, k), lambda i: (i, 0)),
                  pl.BlockSpec((k, d), lambda i: (0, 0)),
                  pl.BlockSpec((tm, d), lambda i: (i, 0)),
                  pl.BlockSpec((1, d), lambda i: (0, 0)),
                  pl.BlockSpec((1, d), lambda i: (0, 0))],
        out_specs=pl.BlockSpec((tm, d), lambda i: (i, 0)),
        out_shape=jax.ShapeDtypeStruct((n, d), F32),
        compiler_params=_cparams(("parallel",)),
        name="proj_ln",
    )(a, w, res, gain.reshape(1, d), bias.reshape(1, d))


def _t5_bucket(dist):
    n = jnp.maximum(dist, 0)
    nf = jnp.maximum(n, 1).astype(F32)
    large = REL_EXACT + (jnp.log(nf / REL_EXACT) / math.log(REL_MAX_DIST / REL_EXACT)
                         * (REL_BUCKETS - REL_EXACT)).astype(jnp.int32)
    return jnp.where(n < REL_EXACT, n, jnp.minimum(large, REL_BUCKETS - 1))


def _bias_lookup(bucket, tab_ref, col):
    out = jnp.zeros(bucket.shape, F32)
    for k in range(REL_BUCKETS):
        out = jnp.where(bucket == k, tab_ref[k, col], out)
    return out


def _per_head_scalar(n_heads, fn):
    hidx = lax.broadcasted_iota(jnp.int32, (n_heads, 1, 1), 0)
    out = jnp.zeros((n_heads, 1, 1), F32)
    for h in range(n_heads):
        out = jnp.where(hidx == h, fn(h), out)
    return out


def _split3_const(value):
    x = np.float32(value)
    hi = np.float32(np.asarray(x, dtype=jnp.bfloat16))
    mid = np.float32(np.asarray(x - hi, dtype=jnp.bfloat16))
    lo = np.float32(np.asarray(x - hi - mid, dtype=jnp.bfloat16))
    return hi, mid, lo


def _split3(x):
    hi = x.astype(BF16)
    r1 = x - hi.astype(F32)
    mid = r1.astype(BF16)
    lo = (r1 - mid.astype(F32)).astype(BF16)
    return hi, mid, lo


def _nsa_compress_kernel(zk_ref, zv_ref, pos_ref, w1_ref, w2_ref, w2t_ref, kc_ref, vct_ref):
    def hidden(idx, z_ref):
        z = z_ref[0, 0]
        u = _dot((z + pos_ref[idx, 0]).astype(BF16), w1_ref[idx, 0])
        v = _dot((z + pos_ref[idx, 1]).astype(BF16), w1_ref[idx, 1])
        return jax.nn.gelu(u + pltpu.roll(v, z.shape[0] - 1, 0)).astype(BF16)

    kc_ref[0, 0] = _dot(hidden(0, zk_ref), w2_ref[0]).astype(kc_ref.dtype)
    vct_ref[0, 0] = _dot_nt(w2t_ref[1], hidden(1, zv_ref)).astype(vct_ref.dtype)


def _nsa_compress(zk, zv, pos, w1, w2):
    b, g, nc, kd = zk.shape
    hid = w1.shape[-1]
    dh = w2.shape[-1]
    zspec = pl.BlockSpec((1, 1, nc, kd), lambda i, j: (i, j, 0, 0))
    return pl.pallas_call(
        _nsa_compress_kernel,
        grid=(b, g),
        in_specs=[zspec, zspec,
                  pl.BlockSpec((2, 2, 1, kd), lambda i, j: (0, 0, 0, 0)),
                  pl.BlockSpec((2, 2, kd, hid), lambda i, j: (0, 0, 0, 0)),
                  pl.BlockSpec((2, hid, dh), lambda i, j: (0, 0, 0)),
                  pl.BlockSpec((2, dh, hid), lambda i, j: (0, 0, 0))],
        out_specs=[pl.BlockSpec((1, 1, nc, dh), lambda i, j: (i, j, 0, 0)),
                   pl.BlockSpec((1, 1, dh, nc), lambda i, j: (i, j, 0, 0))],
        out_shape=[jax.ShapeDtypeStruct((b, g, nc, dh), BF16), jax.ShapeDtypeStruct((b, g, dh, nc), BF16)],
        compiler_params=_cparams(("parallel", "parallel")),
        name="nsa_compress",
    )(zk, zv, pos, w1, w2, w2.transpose(0, 2, 1))


def _nsa_attn_kernel(tab_ref, qt_ref, kc_ref, vct_ref, ks_ref, vst_ref, kw_ref, vwt_ref, gate_ref, o_ref,
                     bn_ref, edge_ref, cb_ref, vt_ref, neg_ref, add_ref, m_ref, acc_ref, mw_ref, accw_ref,
                     s0_ref, s1_ref, m0_ref, m1_ref, *, n_sel):
    tq = NSA_TQ
    tkf = NSA_TKF
    nq = NSA_QPG
    dh = HEAD_DIM
    w = nq * tq
    g = pl.program_id(0)
    iq = pl.program_id(2)
    t0 = iq * tq
    n_cmp = kc_ref.shape[2]
    n_blk = vt_ref.shape[0]
    bpt = tq // SEL_BLOCK
    far_bucket = REL_BUCKETS - 1

    def head_cols(fn):
        return jnp.concatenate([fn(h) for h in range(nq)], axis=1)

    far_row = head_cols(lambda h: jnp.full((1, tq), tab_ref[far_bucket, g * nq + h], F32))

    @pl.when((iq == 0) & (pl.program_id(1) == 0))
    def _init():
        jj = lax.broadcasted_iota(jnp.int32, (tq, tq), 0)
        ii = lax.broadcasted_iota(jnp.int32, (tq, tq), 1)
        d = ii - jj
        b_diag = _t5_bucket(d)
        b_prev = _t5_bucket(d + tq)
        mm = lax.broadcasted_iota(jnp.int32, (NSA_CB_ROWS, tq), 0)
        i2 = lax.broadcasted_iota(jnp.int32, (NSA_CB_ROWS, tq), 1)
        dn = i2 - CMP_STRIDE * mm + (9 * CMP_STRIDE - CMP_LEN + 1)
        b_near = _t5_bucket(dn)
        for h in range(nq):
            col = g * nq + h
            far = tab_ref[far_bucket, col]
            cols = slice(h * tq, (h + 1) * tq)
            bn_ref[0:tq, cols] = _bias_lookup(b_prev, tab_ref, col)
            bn_ref[tq:2 * tq, cols] = jnp.where(d >= 0, _bias_lookup(b_diag, tab_ref, col), NEG_INF)
            edge_ref[:, cols] = jnp.where(jj > ii, far, NEG_INF)
            near = jnp.where(dn >= 0, _bias_lookup(b_near, tab_ref, col), NEG_INF)
            cbv = jnp.where(mm < NSA_NEAR, near,
                            jnp.where(mm == NSA_NEAR, far, jnp.where(mm == NSA_NEAR + 1, NEG_INF, 0.0)))
            hi, mid, lo = _split3(cbv)
            cb_ref[0, :, cols] = hi
            cb_ref[1, :, cols] = mid
            cb_ref[2, :, cols] = lo

    qt = (head_cols(lambda h: qt_ref[0, h]).astype(F32) * (dh ** -0.5)).astype(BF16)

    neg_parts = [float(v) for v in _split3_const(NEG_INF)]
    spare = lax.broadcasted_iota(jnp.int32, (16, 1), 0)
    pad_rows = (jnp.where(spare == 0, neg_parts[0], jnp.where(spare == 1, neg_parts[1],
                                                             jnp.where(spare == 2, neg_parts[2], 0.0)))
                + jnp.zeros((16, w), F32)).astype(BF16)

    def aug_rhs(pieces):
        return jnp.concatenate([qt] + list(pieces) + [pad_rows], axis=0)

    zero16 = jnp.zeros((16, w), BF16)
    rhs_plain = aug_rhs([zero16, zero16, zero16])

    def kv_start(key0):
        return pl.multiple_of(key0 + tq, LANE)

    def flash_first(m_st, acc_st, s, vt):
        m = jnp.max(s, axis=0, keepdims=True)
        m_st[...] = m
        acc_st[...] = _dot(vt, jnp.exp(s - m).astype(BF16))

    def flash_next(m_st, acc_st, s, s_max, vt):
        m_old = m_st[...]
        m_new = jnp.maximum(m_old, s_max)
        acc_st[...] = jnp.exp(m_old - m_new) * acc_st[...] + _dot(vt, jnp.exp(s - m_new).astype(BF16))
        m_st[...] = m_new

    def flash_out(acc_st):
        acc = acc_st[...]
        return acc[0:dh] * (1.0 / acc[dh:dh + 1])

    n_lo = t0 // CMP_STRIDE - 9
    ni = lax.broadcasted_iota(jnp.int32, (n_cmp, NSA_CB_ROWS), 0)
    mi = lax.broadcasted_iota(jnp.int32, (n_cmp, NSA_CB_ROWS), 1)
    place = (((mi < NSA_NEAR) & (ni == mi + n_lo)) | ((mi == NSA_NEAR) & (ni < n_lo))
             | ((mi == NSA_NEAR + 1) & (ni >= n_lo + NSA_NEAR)))
    shift = jnp.where(place, 1.0, 0.0).astype(BF16)
    lhs_c = jnp.concatenate([kc_ref[0, 0], shift, shift, shift], axis=1)
    rhs_c = jnp.concatenate([qt, cb_ref[0], cb_ref[1], cb_ref[2]], axis=0)
    lcm = _dot(lhs_c, rhs_c)
    ec = jnp.exp(lcm - jnp.max(lcm, axis=0, keepdims=True))
    sc = jnp.sum(ec, axis=0, keepdims=True)
    tl = t0 + lax.broadcasted_iota(jnp.int32, (1, w), 1) % tq
    pct = ec * jnp.where(tl >= CMP_LEN - 1, 1.0 / sc, 0.0)
    o_ct = _dot(vct_ref[0, 0], pct.astype(BF16))

    psum = pct[:, 0:tq]
    for h in range(1, nq):
        psum = psum + pct[:, h * tq:(h + 1) * tq]
    p_hi = psum.astype(BF16)
    p_lo = (psum - p_hi.astype(F32)).astype(BF16)
    jb = lax.broadcasted_iota(jnp.int32, (n_blk, n_cmp), 0)
    nb = lax.broadcasted_iota(jnp.int32, (n_blk, n_cmp), 1)
    ovl = jnp.where((CMP_STRIDE * nb < SEL_BLOCK * (jb + 1)) & (CMP_STRIDE * nb + CMP_LEN > SEL_BLOCK * jb),
                    1.0, 0.0).astype(BF16)
    imp_t = _dot(ovl, p_hi) + _dot(ovl, p_lo)
    jb2 = lax.broadcasted_iota(jnp.int32, (n_blk, tq), 0)
    cur = (t0 + lax.broadcasted_iota(jnp.int32, (n_blk, tq), 1)) // SEL_BLOCK
    forced = (jb2 == 0) | (jb2 == cur) | (jb2 == cur - 1)
    val = jnp.where(jb2 > cur, -1e9, jnp.where(forced, 1e9, imp_t))
    vt_ref[...] = val

    def rank_body(jp, rank):
        row = vt_ref[pl.ds(jp, 1), :]
        beats = (row > val) | ((row == val) & (jp < jb2))
        return rank + jnp.where(beats, 1, 0)

    n_causal = (t0 + tq) // SEL_BLOCK
    rank = lax.fori_loop(0, jnp.where(n_causal <= n_sel, 0, n_causal), rank_body,
                         jnp.zeros((n_blk, tq), jnp.int32))
    neg_t = jnp.where(rank < n_sel, 0.0, NEG_INF)
    neg4 = jnp.concatenate([neg_t] * nq, axis=1)
    neg_ref[...] = neg4

    blk0 = iq * bpt
    n_far = jnp.maximum(blk0 - bpt, 0)

    def split_rows(masked, live_row):
        return [jnp.where(masked, neg_parts[i], piece.astype(F32)) for i, piece in enumerate(_split3(live_row))]

    brow = lax.broadcasted_iota(jnp.int32, (n_blk, 1), 0)
    far_parts = split_rows((neg4 < 0.0) | (brow >= n_far), far_row)
    for part in range(3):
        add_ref[part] = far_parts[part].reshape(n_blk // 16, 16, w).astype(BF16)

    near_rows = []
    for j in range(16):
        off = (j - (blk0 - bpt)) % 16
        row = neg_ref[pl.ds(jnp.clip(blk0 - bpt + off, 0, n_blk - 1), 1), :]
        near_rows.append(jnp.where(off < 2 * bpt, row, 0.0))
    near_mask = jnp.concatenate(near_rows, axis=0) < 0.0
    rhs_near = aug_rhs([p.astype(BF16) for p in split_rows(near_mask, jnp.zeros((1, w), F32))])

    near0 = kv_start(t0 - tq)
    edge0 = kv_start(jnp.maximum(t0 - 2 * tq, -tq))
    s_win = _dot(kw_ref[0, 0, pl.ds(near0, 2 * tq), :], rhs_plain) + bn_ref[...]
    s_edge = _dot(kw_ref[0, 0, pl.ds(edge0, tq), :], rhs_plain) + edge_ref[...]
    flash_first(mw_ref, accw_ref, s_win, vwt_ref[0, 0, :, pl.ds(near0, 2 * tq)])
    flash_next(mw_ref, accw_ref, s_edge, jnp.max(s_edge, axis=0, keepdims=True), vwt_ref[0, 0, :, pl.ds(edge0, tq)])
    o_wt = flash_out(accw_ref)

    s_sel = _dot(ks_ref[0, 0, pl.ds(near0, 2 * tq), :], rhs_near) + bn_ref[...]
    flash_first(m_ref, acc_ref, s_sel, vst_ref[0, 0, :, pl.ds(near0, 2 * tq)])

    bpf = tkf // SEL_BLOCK
    n_tiles = n_far // bpf

    def qk_stage(c, s_buf, m_buf):
        chunk = (c * bpf) // 16
        s = _dot(ks_ref[0, 0, pl.ds(kv_start(c * tkf), tkf), :],
                 aug_rhs([add_ref[0, chunk], add_ref[1, chunk], add_ref[2, chunk]]))
        s_buf[...] = s
        m_buf[...] = jnp.max(s, axis=0, keepdims=True)

    def sm_stage(c, s_buf, m_buf):
        flash_next(m_ref, acc_ref, s_buf[...], m_buf[...], vst_ref[0, 0, :, pl.ds(kv_start(c * tkf), tkf)])

    @pl.when(n_tiles > 0)
    def _():
        qk_stage(0, s0_ref, m0_ref)

    def far_body(i, carry):
        qk_stage(2 * i + 1, s1_ref, m1_ref)
        sm_stage(2 * i, s0_ref, m0_ref)
        qk_stage(2 * i + 2, s0_ref, m0_ref)
        sm_stage(2 * i + 1, s1_ref, m1_ref)
        return carry

    lax.fori_loop(0, (n_tiles + 1) // 2, far_body, 0)
    o_st = flash_out(acc_ref)

    sg = jax.nn.sigmoid(gate_ref[0, 0])
    outs = []
    for h in range(nq):
        cols = slice(h * tq, (h + 1) * tq)
        outs.append(sg[3 * h:3 * h + 1] * o_ct[:, cols] + sg[3 * h + 1:3 * h + 2] * o_st[:, cols]
                    + sg[3 * h + 2:3 * h + 3] * o_wt[:, cols])
    o_ref[0] = jnp.concatenate(outs, axis=0).T.astype(o_ref.dtype)


def _nsa_attention(tab, qt, kc, vct, ks, vst, kw, vwt, gate):
    b, _, dh, t = qt.shape
    g = NSA_KV_HEADS
    nq = NSA_QPG
    tq = NSA_TQ
    w = nq * tq
    assert t % tq == 0 and tq >= REL_MAX_DIST and NSA_WINDOW == 2 * tq and tq % SEL_BLOCK == 0
    assert NSA_TKF % SEL_BLOCK == 0 and t >= NSA_TKF and tq % LANE == 0
    assert NSA_NEAR + 1 < NSA_CB_ROWS
    n_cmp = kc.shape[2]
    n_blk = t // SEL_BLOCK
    n_sel = min(SEL_COUNT, n_blk)
    assert n_sel >= 3
    tp = ks.shape[2]
    bpf = NSA_TKF // SEL_BLOCK
    assert tp == t + tq and vst.shape[3] == tp and 16 % bpf == 0 and n_blk % 16 == 0 and tq % NSA_TKF == 0
    assert ks.shape[3] == 2 * dh and vst.shape[2] == dh + 16
    kspec = pl.BlockSpec((1, 1, tp, 2 * dh), lambda j, i, k: (i, j, 0, 0))
    vtspec = pl.BlockSpec((1, 1, dh + 16, tp), lambda j, i, k: (i, j, 0, 0))
    return pl.pallas_call(
        functools.partial(_nsa_attn_kernel, n_sel=n_sel),
        grid=(g, b, t // tq),
        in_specs=[pl.BlockSpec(memory_space=pltpu.SMEM),
                  pl.BlockSpec((1, nq, dh, tq), lambda j, i, k: (i, j, 0, k)),
                  pl.BlockSpec((1, 1, n_cmp, dh), lambda j, i, k: (i, j, 0, 0)),
                  pl.BlockSpec((1, 1, dh, n_cmp), lambda j, i, k: (i, j, 0, 0)),
                  kspec, vtspec, kspec, vtspec,
                  pl.BlockSpec((1, 1, 16, tq), lambda j, i, k: (i, j, 0, k))],
        out_specs=pl.BlockSpec((1, tq, nq * dh), lambda j, i, k: (i, k, j)),
        out_shape=jax.ShapeDtypeStruct((b, t, g * nq * dh), BF16),
        scratch_shapes=[pltpu.VMEM((2 * tq, w), F32),
                        pltpu.VMEM((tq, w), F32),
                        pltpu.VMEM((3, NSA_CB_ROWS, w), BF16),
                        pltpu.VMEM((n_blk, tq), F32),
                        pltpu.VMEM((n_blk, w), F32),
                        pltpu.VMEM((3, n_blk // 16, 16, w), BF16),
                        pltpu.VMEM((1, w), F32),
                        pltpu.VMEM((dh + 16, w), F32),
                        pltpu.VMEM((1, w), F32),
                        pltpu.VMEM((dh + 16, w), F32),
                        pltpu.VMEM((NSA_TKF, w), F32), pltpu.VMEM((NSA_TKF, w), F32),
                        pltpu.VMEM((1, w), F32), pltpu.VMEM((1, w), F32)],
        compiler_params=_cparams(("arbitrary", "arbitrary", "arbitrary")),
        name="nsa_attn",
    )(tab, qt, kc, vct, ks, vst, kw, vwt, gate)


def _nsa_layer(x, w_in, w_out, cmp_pos, cmp_w1, cmp_w2, rel_bias, gain, bias, alpha):
    b, t, d = x.shape
    g, nq, dh = NSA_KV_HEADS, NSA_QPG, HEAD_DIM
    kvw = g * dh
    x2 = x.reshape(b * t, d)
    off = ATTN_HEADS * dh
    cols_bf = np.concatenate([np.arange(off), np.arange(off + 2 * kvw, off + 6 * kvw)])
    cols_f = np.concatenate([np.arange(off, off + 2 * kvw), np.arange(off + 6 * kvw, off + 6 * kvw + 3 * ATTN_HEADS)])
    n_pad = -len(cols_f) % LANE
    w_bf = w_in[:, cols_bf].astype(BF16)
    w_f = jnp.pad(w_in[:, cols_f], ((0, 0), (0, n_pad))).astype(BF16)
    hb = _proj(x2, w_bf, tn=512, out_dtype=BF16).reshape(b, t, len(cols_bf))
    hf = _proj(x2, w_f, tn=len(cols_f) + n_pad).reshape(b, t, len(cols_f) + n_pad)

    def heads(src, lo, n_heads, perm=(0, 2, 1, 3)):
        return src[:, :, lo:lo + n_heads * dh].reshape(b, t, n_heads, dh).transpose(perm)

    head_t = (0, 2, 3, 1)
    qt = heads(hb, 0, ATTN_HEADS, head_t)
    zk = heads(hf, 0, g).reshape(b, g, t // CMP_STRIDE, CMP_STRIDE * dh)
    zv = heads(hf, kvw, g).reshape(b, g, t // CMP_STRIDE, CMP_STRIDE * dh)
    tp = t + NSA_TQ
    is_pad = np.arange(tp) < NSA_TQ
    slot = (np.arange(tp) - NSA_TQ) // SEL_BLOCK % 16
    onehot = ((np.arange(16)[None, :] == slot[:, None]) & ~is_pad[:, None]).astype(np.float32)
    pad_cols = ((np.arange(16)[None, :] < 3) & is_pad[:, None]).astype(np.float32)
    k_cols = jnp.asarray(np.concatenate([onehot] * 3 + [pad_cols], axis=1), BF16)
    v_rows = jnp.asarray((np.arange(16)[:, None] == 0) * np.ones((1, tp)), BF16)

    def aug_k(z):
        z = jnp.pad(z, ((0, 0), (0, 0), (NSA_TQ, 0), (0, 0)))
        return jnp.concatenate([z, jnp.broadcast_to(k_cols, (b, g, tp, dh))], axis=3)

    def aug_vt(z):
        z = jnp.pad(z, ((0, 0), (0, 0), (0, 0), (NSA_TQ, 0)))
        return jnp.concatenate([z, jnp.broadcast_to(v_rows, (b, g, 16, tp))], axis=2)

    ks = aug_k(heads(hb, off, g))
    vst = aug_vt(heads(hb, off + kvw, g, head_t))
    kw = aug_k(heads(hb, off + 2 * kvw, g))
    vwt = aug_vt(heads(hb, off + 3 * kvw, g, head_t))
    gate = hf[:, :, 2 * kvw:2 * kvw + 3 * ATTN_HEADS].reshape(b, t, g, 3 * nq)
    gate = jnp.pad(gate.transpose(0, 2, 3, 1), ((0, 0), (0, 0), (0, 16 - 3 * nq), (0, 0)))

    pos = cmp_pos.reshape(2, 2, 1, CMP_STRIDE * dh)
    w1 = cmp_w1.reshape(2, 2, CMP_STRIDE * dh, cmp_w1.shape[-1]).astype(BF16)
    kc, vct = _nsa_compress(zk, zv, pos, w1, cmp_w2.astype(BF16))
    o = _nsa_attention(rel_bias, qt, kc, vct, ks, vst, kw, vwt, gate)
    y = _proj_ln(o.reshape(b * t, d), w_out.astype(BF16), x2, gain, bias, alpha)
    return y.reshape(b, t, d)


def _hgrn_kernel(lbp_ref, gain_ref, zq_ref, zf_ref, zi_ref, zg_ref, o_ref, st_ref, *, layer):
    C = HGRN_TC
    t = zq_ref.shape[1]
    kdim = zq_ref.shape[2]

    p = lbp_ref[...]
    e = jnp.exp(p - jnp.max(p, axis=0, keepdims=True))
    sm = e / jnp.sum(e, axis=0, keepdims=True)
    cs = sm[0:1]
    for r in range(1, layer + 1):
        cs = cs + sm[r:r + 1]
    lb = cs - sm[0:1]
    log_lb = jnp.log(lb)
    log_1m = jnp.log1p(-lb)
    gain = gain_ref[...]

    st_ref[...] = jnp.zeros(st_ref.shape, F32)

    rr = lax.broadcasted_iota(jnp.int32, (C, C), 0)
    cc = lax.broadcasted_iota(jnp.int32, (C, C), 1)
    srow = lax.broadcasted_iota(jnp.int32, (C, 1), 0)
    halves = [1 << i for i in range(C.bit_length() - 1)]
    small = [h for h in halves if h < 8]
    sel_rows = [jnp.where(cc <= rr, 1.0, 0.0).astype(BF16)]
    sel_rows += [jnp.where(cc <= (rr // (2 * h)) * (2 * h) + h - 1, 1.0, 0.0).astype(BF16) for h in small]
    cum_sel = jnp.concatenate(sel_rows, axis=0)
    right = [(srow // h) % 2 == 1 for h in halves]
    same_blk = [rr // (2 * h) == cc // (2 * h) for h in halves]
    eye = rr == cc

    def chunk(c, carry):
        r0 = pl.multiple_of(c * C, C)
        zq = zq_ref[0, pl.ds(r0, C), :]
        zf = zf_ref[0, pl.ds(r0, C), :]
        v = zi_ref[0, pl.ds(r0, C), :]
        zg = zg_ref[0, pl.ds(r0, C), :]
        q = jax.nn.silu(zq)
        log_f = jnp.logaddexp(log_lb, log_1m + jax.nn.log_sigmoid(zf))
        kk = (1.0 - lb) * jax.nn.sigmoid(-zf)
        vb = v.astype(BF16)

        g_hi, g_mid, g_lo = _split3(log_f)
        cums = _dot(cum_sel, g_hi) + _dot(cum_sel, g_mid) + _dot(cum_sel, g_lo)
        bcum = cums[0:C]
        b_last = bcum[C - 1:C, :]

        st = st_ref[...]
        o = _dot_nt((q * jnp.exp(bcum)).astype(BF16), st.astype(BF16))

        a = jnp.where(eye, _dot_nt(q.astype(BF16), kk.astype(BF16)), 0.0)
        for lvl, h in enumerate(halves):
            if h in small:
                b_ref = cums[(lvl + 1) * C:(lvl + 2) * C]
            else:
                b_ref = jnp.broadcast_to(bcum.reshape(C // (2 * h), 2 * h, kdim)[:, h - 1:h, :],
                                         (C // (2 * h), 2 * h, kdim)).reshape(C, kdim)
            e = jnp.exp(-jnp.abs(bcum - b_ref))
            q_side = jnp.where(right[lvl], q * e, 0.0)
            k_side = jnp.where(right[lvl], 0.0, kk * e)
            a = a + jnp.where(same_blk[lvl], _dot_nt(q_side.astype(BF16), k_side.astype(BF16)), 0.0)
        o = o + _dot(a.astype(BF16), vb)

        kd = kk * jnp.exp(b_last - bcum)
        st_ref[...] = st * jnp.exp(b_last) + _dot(v.T.astype(BF16), kd.astype(BF16))

        o = o * lax.rsqrt(jnp.mean(o * o, axis=-1, keepdims=True) + RMS_EPS) * gain
        o = o * jax.nn.silu(zg)
        o_ref[0, pl.ds(r0, C), :] = o.astype(o_ref.dtype)
        return carry

    lax.fori_loop(0, t // C, chunk, 0, unroll=4)


def _hgrn_layer(x, w_in, w_out, norm_gain, lb_param, layer, gain, bias, alpha):
    b, t, d = x.shape
    nh = HGRN_HEADS
    kd = d // nh
    x2 = x.reshape(b * t, d)
    h = _proj(x2, w_in.astype(BF16), tn=1024).reshape(b, t, 4 * d)

    def zspec(part):
        return pl.BlockSpec((1, t, kd), lambda i, j: (i, 0, part * nh + j))

    o = pl.pallas_call(
        functools.partial(_hgrn_kernel, layer=layer),
        grid=(b, nh),
        in_specs=[pl.BlockSpec((lb_param.shape[0], kd), lambda i, j: (0, j)),
                  pl.BlockSpec((1, kd), lambda i, j: (0, 0)),
                  zspec(0), zspec(1), zspec(2), zspec(3)],
        out_specs=pl.BlockSpec((1, t, kd), lambda i, j: (i, 0, j)),
        out_shape=jax.ShapeDtypeStruct((b, t, d), BF16),
        scratch_shapes=[pltpu.VMEM((kd, kd), F32)],
        compiler_params=_cparams(("parallel", "parallel")),
        name="hgrn",
    )(lb_param, norm_gain.reshape(1, kd), h, h, h, h)
    y = _proj_ln(o.reshape(b * t, d), w_out.astype(BF16), x2, gain, bias, alpha)
    return y.reshape(b, t, d)


def _swa_kernel(tab_ref, sink_ref, q_ref, k_ref, v_ref, o_ref, bd_ref, bp_ref):
    L = SWA_WINDOW
    nq = SWA_QPG
    dh = HEAD_DIM
    g = pl.program_id(1)
    n = pl.program_id(2)
    ii = lax.broadcasted_iota(jnp.int32, (L, L), 0)
    jj = lax.broadcasted_iota(jnp.int32, (L, L), 1)

    @pl.when(n == 0)
    def _init():
        b_diag = _t5_bucket(ii - jj)
        b_prev = _t5_bucket(ii - jj + L)
        for h in range(nq):
            bd_ref[h] = _bias_lookup(b_diag, tab_ref, g * nq + h)
            bp_ref[h] = _bias_lookup(b_prev, tab_ref, g * nq + h)

    qs = (q_ref[0].astype(F32) * (dh ** -0.5)).astype(BF16).reshape(nq * L, dh)
    cur = pl.multiple_of(n * L, L)
    prev = pl.multiple_of(jnp.maximum(n - 1, 0) * L, L)
    k_c = k_ref[0, 0, pl.ds(cur, L), :]
    v_c = v_ref[0, 0, pl.ds(cur, L), :]
    k_p = k_ref[0, 0, pl.ds(prev, L), :]
    v_p = v_ref[0, 0, pl.ds(prev, L), :]

    s_c = _dot_nt(qs, k_c).reshape(nq, L, L) + bd_ref[...]
    s_p = _dot_nt(qs, k_p).reshape(nq, L, L) + bp_ref[...]
    s_c = jnp.where((ii >= jj)[None], s_c, NEG_INF)
    s_p = jnp.where(((jj > ii) & (n > 0))[None], s_p, NEG_INF)
    sink = _per_head_scalar(nq, lambda h: sink_ref[g * nq + h])
    m = jnp.maximum(jnp.maximum(jnp.max(s_c, axis=-1, keepdims=True), jnp.max(s_p, axis=-1, keepdims=True)), sink)
    e_c = jnp.exp(s_c - m)
    e_p = jnp.exp(s_p - m)
    den = jnp.sum(e_c, axis=-1, keepdims=True) + jnp.sum(e_p, axis=-1, keepdims=True) + jnp.exp(sink - m)
    pv = (_dot(e_c.reshape(nq * L, L).astype(BF16), v_c) + _dot(e_p.reshape(nq * L, L).astype(BF16), v_p))
    o = pv.reshape(nq, L, dh) / den
    o_ref[0] = jnp.concatenate([o[h] for h in range(nq)], axis=1).astype(o_ref.dtype)


def _swa_layer(x, w_in, w_out, sinks, rel_bias, gain, bias, alpha):
    b, t, d = x.shape
    kvh, nq, dh, L = SWA_KV_HEADS, SWA_QPG, HEAD_DIM, SWA_WINDOW
    assert t % L == 0 and L >= REL_MAX_DIST
    x2 = x.reshape(b * t, d)
    n_in = w_in.shape[1]
    h = _proj(x2, w_in.astype(BF16), tn=n_in // 2, out_dtype=BF16).reshape(b, t, n_in)

    def heads(lo, n_heads):
        return h[:, :, lo:lo + n_heads * dh].reshape(b, t, n_heads, dh).transpose(0, 2, 1, 3)

    q = heads(0, ATTN_HEADS)
    k = heads(ATTN_HEADS * dh, kvh)
    v = heads(ATTN_HEADS * dh + kvh * dh, kvh)
    kvspec = pl.BlockSpec((1, 1, t, dh), lambda i, j, n: (i, j, 0, 0))
    o = pl.pallas_call(
        _swa_kernel,
        grid=(b, kvh, t // L),
        in_specs=[pl.BlockSpec(memory_space=pltpu.SMEM),
                  pl.BlockSpec(memory_space=pltpu.SMEM),
                  pl.BlockSpec((1, nq, L, dh), lambda i, j, n: (i, j, n, 0)),
                  kvspec, kvspec],
        out_specs=pl.BlockSpec((1, L, nq * dh), lambda i, j, n: (i, n, j)),
        out_shape=jax.ShapeDtypeStruct((b, t, d), BF16),
        scratch_shapes=[pltpu.VMEM((nq, L, L), F32), pltpu.VMEM((nq, L, L), F32)],
        compiler_params=_cparams(("parallel", "parallel", "arbitrary")),
        name="swa_attn",
    )(rel_bias, sinks, q, k, v)
    y = _proj_ln(o.reshape(b * t, d), w_out.astype(BF16), x2, gain, bias, alpha)
    return y.reshape(b, t, d)


def kernel(x, rel_bias, ln_gain, ln_bias, ffn1_w_gate, ffn1_w_up, ffn1_w_down, ffn2_w_gate, ffn2_w_up,
           ffn2_w_down, nsa_w_in, nsa_w_out, nsa_cmp_pos, nsa_cmp_w1, nsa_cmp_w2, hgrn_w_in, hgrn_w_out,
           hgrn_norm_gain, hgrn_lb, swa_w_in, swa_w_out, swa_sinks):
    depth = ln_gain.shape[0]
    alpha = (2.0 * depth) ** 0.25
    b, t, d = x.shape

    def ffn(x, wg, wu, wd, gain, bias):
        y = _ffn_ln(x.reshape(b * t, d), wg.astype(BF16), wu.astype(BF16), wd.astype(BF16), gain, bias, alpha)
        return y.reshape(b, t, d)

    for i in range(depth):
        x = ffn(x, ffn1_w_gate[i], ffn1_w_up[i], ffn1_w_down[i], ln_gain[i, 0], ln_bias[i, 0])
        kind, slot = i % N_MIXERS, i // N_MIXERS
        if kind == 0:
            x = _nsa_layer(x, nsa_w_in[slot], nsa_w_out[slot], nsa_cmp_pos[slot], nsa_cmp_w1[slot],
                           nsa_cmp_w2[slot], rel_bias, ln_gain[i, 1], ln_bias[i, 1], alpha)
        elif kind == 1:
            x = _hgrn_layer(x, hgrn_w_in[slot], hgrn_w_out[slot], hgrn_norm_gain[slot], hgrn_lb, i,
                            ln_gain[i, 1], ln_bias[i, 1], alpha)
        else:
            x = _swa_layer(x, swa_w_in[slot], swa_w_out[slot], swa_sinks[slot], rel_bias,
                           ln_gain[i, 1], ln_bias[i, 1], alpha)
        x = ffn(x, ffn2_w_gate[i], ffn2_w_up[i], ffn2_w_down[i], ln_gain[i, 2], ln_bias[i, 2])
    return x
```

```python
import functools
import math

import jax
import jax.numpy as jnp
import numpy as np
from jax import lax
from jax.experimental import pallas as pl
from jax.experimental.pallas import tpu as pltpu

F32 = jnp.float32
BF16 = jnp.bfloat16

DEPTH = 4
N_MIXERS = 3
REL_BUCKETS = 32
REL_EXACT = REL_BUCKETS // 2
REL_MAX_DIST = 128
ATTN_HEADS = 16
HEAD_DIM = 64
NSA_KV_HEADS = 4
NSA_QPG = ATTN_HEADS // NSA_KV_HEADS
CMP_STRIDE = 16
CMP_LEN = 2 * CMP_STRIDE
SEL_BLOCK = 64
SEL_COUNT = 16
NSA_WINDOW = 512
HGRN_HEADS = 8
HGRN_CHUNK = 64
HGRN_TC = 128
SWA_KV_HEADS = 2
SWA_QPG = ATTN_HEADS // SWA_KV_HEADS
SWA_WINDOW = 128
LN_EPS = 1e-5
RMS_EPS = 1e-6
NEG_INF = -1e30
LOG2E = 1.4426950408889634

LANE = 128
VMEM_LIMIT = 48 * 1024 * 1024
NSA_TQ = 256
NSA_TKF = 256
NSA_NEAR = NSA_TQ // CMP_STRIDE + 8
NSA_CB_ROWS = 32


def _cparams(sem):
    return pltpu.CompilerParams(dimension_semantics=sem, vmem_limit_bytes=VMEM_LIMIT)


def _dot(a, b):
    return jnp.dot(a, b, preferred_element_type=F32)


def _dot_nt(a, b):
    return lax.dot_general(a, b, (((1,), (1,)), ((), ())), preferred_element_type=F32)


def _layer_norm_rows(y, g, b):
    mu = jnp.mean(y, axis=-1, keepdims=True)
    yc = y - mu
    var = jnp.mean(yc * yc, axis=-1, keepdims=True)
    return yc * lax.rsqrt(var + LN_EPS) * g + b


def _ffn_ln_kernel(x_ref, wg_ref, wu_ref, wd_ref, g_ref, b_ref, o_ref, xb_ref, acc_ref, *, alpha):
    j = pl.program_id(1)

    @pl.when(j == 0)
    def _():
        xb_ref[...] = x_ref[...].astype(BF16)
        acc_ref[...] = jnp.zeros_like(acc_ref)

    xb = xb_ref[...]
    gate = _dot(xb, wg_ref[...])
    up = _dot(xb, wu_ref[...])
    h = (jax.nn.silu(gate) * up).astype(BF16)
    acc_ref[...] += _dot(h, wd_ref[...])

    @pl.when(j == pl.num_programs(1) - 1)
    def _():
        y = alpha * x_ref[...] + 0.5 * acc_ref[...]
        o_ref[...] = _layer_norm_rows(y, g_ref[...], b_ref[...])


def _ffn_ln(x, wg, wu, wd, gain, bias, alpha, tm=512, tf=1408):
    n, d = x.shape
    f = wg.shape[1]
    tm = min(tm, n)
    assert n % tm == 0 and f % tf == 0
    return pl.pallas_call(
        functools.partial(_ffn_ln_kernel, alpha=alpha),
        grid=(n // tm, f // tf),
        in_specs=[
            pl.BlockSpec((tm, d), lambda i, j: (i, 0)),
            pl.BlockSpec((d, tf), lambda i, j: (0, j)),
            pl.BlockSpec((d, tf), lambda i, j: (0, j)),
            pl.BlockSpec((tf, d), lambda i, j: (j, 0)),
            pl.BlockSpec((1, d), lambda i, j: (0, 0)),
            pl.BlockSpec((1, d), lambda i, j: (0, 0)),
        ],
        out_specs=pl.BlockSpec((tm, d), lambda i, j: (i, 0)),
        out_shape=jax.ShapeDtypeStruct((n, d), F32),
        scratch_shapes=[pltpu.VMEM((tm, d), BF16), pltpu.VMEM((tm, d), F32)],
        compiler_params=_cparams(("parallel", "arbitrary")),
        name="ffn_ln",
    )(x, wg, wu, wd, gain.reshape(1, d), bias.reshape(1, d))


def _proj_kernel(x_ref, w_ref, o_ref, xb_ref):
    @pl.when(pl.program_id(1) == 0)
    def _():
        xb_ref[...] = x_ref[...].astype(BF16)

    o_ref[...] = _dot(xb_ref[...], w_ref[...]).astype(o_ref.dtype)


def _proj(x, w, tn, tm=1024, out_dtype=F32):
    n, k = x.shape
    m = w.shape[1]
    tm = min(tm, n)
    assert n % tm == 0 and m % tn == 0
    return pl.pallas_call(
        _proj_kernel,
        grid=(n // tm, m // tn),
        in_specs=[pl.BlockSpec((tm, k), lambda i, j: (i, 0)),
                  pl.BlockSpec((k, tn), lambda i, j: (0, j))],
        out_specs=pl.BlockSpec((tm, tn), lambda i, j: (i, j)),
        out_shape=jax.ShapeDtypeStruct((n, m), out_dtype),
        scratch_shapes=[pltpu.VMEM((tm, k), BF16)],
        compiler_params=_cparams(("parallel", "arbitrary")),
        name="proj",
    )(x, w)


def _proj_t_kernel(x_ref, wt_ref, o_ref, xb_ref):
    @pl.when(pl.program_id(2) == 0)
    def _():
        xb_ref[...] = x_ref[0].astype(BF16)

    o_ref[0] = _dot_nt(wt_ref[...], xb_ref[...]).astype(o_ref.dtype)


def _proj_t(x, wt, tn, tm=1024, out_dtype=BF16):
    b, t, k = x.shape
    m = wt.shape[0]
    tm = min(tm, t)
    assert t % tm == 0 and m % tn == 0
    return pl.pallas_call(
        _proj_t_kernel,
        grid=(b, t // tm, m // tn),
        in_specs=[pl.BlockSpec((1, tm, k), lambda i, j, n: (i, j, 0)),
                  pl.BlockSpec((tn, k), lambda i, j, n: (n, 0))],
        out_specs=pl.BlockSpec((1, tn, tm), lambda i, j, n: (i, n, j)),
        out_shape=jax.ShapeDtypeStruct((b, m, t), out_dtype),
        scratch_shapes=[pltpu.VMEM((tm, k), BF16)],
        compiler_params=_cparams(("parallel", "parallel", "arbitrary")),
        name="proj_t",
    )(x, wt)


def _proj_ln_kernel(a_ref, w_ref, r_ref, g_ref, b_ref, o_ref, *, alpha):
    y = _dot(a_ref[...].astype(BF16), w_ref[...])
    o_ref[...] = _layer_norm_rows(alpha * r_ref[...] + y, g_ref[...], b_ref[...])


def _proj_ln(a, w, res, gain, bias, alpha, tm=512):
    n, k = a.shape
    d = w.shape[1]
    tm = min(tm, n)
    assert n % tm == 0
    return pl.pallas_call(
        functools.partial(_proj_ln_kernel, alpha=alpha),
        grid=(n // tm,),
        in_specs=[pl.BlockSpec((tm, k), lambda i: (i, 0)),
                  pl.BlockSpec((k, d), lambda i: (0, 0)),
                  pl.BlockSpec((tm, d), lambda i: (i, 0)),
                  pl.BlockSpec((1, d), lambda i: (0, 0)),
                  pl.BlockSpec((1, d), lambda i: (0, 0))],
        out_specs=pl.BlockSpec((tm, d), lambda i: (i, 0)),
        out_shape=jax.ShapeDtypeStruct((n, d), F32),
        compiler_params=_cparams(("parallel",)),
        name="proj_ln",
    )(a, w, res, gain.reshape(1, d), bias.reshape(1, d))


def _t5_bucket(dist):
    n = jnp.maximum(dist, 0)
    nf = jnp.maximum(n, 1).astype(F32)
    large = REL_EXACT + (jnp.log(nf / REL_EXACT) / math.log(REL_MAX_DIST / REL_EXACT)
                         * (REL_BUCKETS - REL_EXACT)).astype(jnp.int32)
    return jnp.where(n < REL_EXACT, n, jnp.minimum(large, REL_BUCKETS - 1))


def _bias_lookup(bucket, tab_ref, col):
    out = jnp.zeros(bucket.shape, F32)
    for k in range(REL_BUCKETS):
        out = jnp.where(bucket == k, tab_ref[k, col], out)
    return out


def _per_head_scalar(n_heads, fn):
    hidx = lax.broadcasted_iota(jnp.int32, (n_heads, 1, 1), 0)
    out = jnp.zeros((n_heads, 1, 1), F32)
    for h in range(n_heads):
        out = jnp.where(hidx == h, fn(h), out)
    return out


def _split3_const(value):
    x = np.float32(value)
    hi = np.float32(np.asarray(x, dtype=jnp.bfloat16))
    mid = np.float32(np.asarray(x - hi, dtype=jnp.bfloat16))
    lo = np.float32(np.asarray(x - hi - mid, dtype=jnp.bfloat16))
    return hi, mid, lo


def _split3(x):
    hi = x.astype(BF16)
    r1 = x - hi.astype(F32)
    mid = r1.astype(BF16)
    lo = (r1 - mid.astype(F32)).astype(BF16)
    return hi, mid, lo


def _nsa_compress_kernel(zk_ref, zv_ref, pos_ref, w1_ref, w2_ref, w2t_ref, kc_ref, vct_ref):
    def hidden(idx, z_ref):
        z = z_ref[0, 0]
        u = _dot((z + pos_ref[idx, 0]).astype(BF16), w1_ref[idx, 0])
        v = _dot((z + pos_ref[idx, 1]).astype(BF16), w1_ref[idx, 1])
        return jax.nn.gelu(u + pltpu.roll(v, z.shape[0] - 1, 0)).astype(BF16)

    kc_ref[0, 0] = _dot(hidden(0, zk_ref), w2_ref[0]).astype(kc_ref.dtype)
    vct_ref[0, 0] = _dot_nt(w2t_ref[1], hidden(1, zv_ref)).astype(vct_ref.dtype)


def _nsa_compress(zk, zv, pos, w1, w2):
    b, g, nc, kd = zk.shape
    hid = w1.shape[-1]
    dh = w2.shape[-1]
    zspec = pl.BlockSpec((1, 1, nc, kd), lambda i, j: (i, j, 0, 0))
    return pl.pallas_call(
        _nsa_compress_kernel,
        grid=(b, g),
        in_specs=[zspec, zspec,
                  pl.BlockSpec((2, 2, 1, kd), lambda i, j: (0, 0, 0, 0)),
                  pl.BlockSpec((2, 2, kd, hid), lambda i, j: (0, 0, 0, 0)),
                  pl.BlockSpec((2, hid, dh), lambda i, j: (0, 0, 0)),
                  pl.BlockSpec((2, dh, hid), lambda i, j: (0, 0, 0))],
        out_specs=[pl.BlockSpec((1, 1, nc, dh), lambda i, j: (i, j, 0, 0)),
                   pl.BlockSpec((1, 1, dh, nc), lambda i, j: (i, j, 0, 0))],
        out_shape=[jax.ShapeDtypeStruct((b, g, nc, dh), BF16), jax.ShapeDtypeStruct((b, g, dh, nc), BF16)],
        compiler_params=_cparams(("parallel", "parallel")),
        name="nsa_compress",
    )(zk, zv, pos, w1, w2, w2.transpose(0, 2, 1))


def _nsa_attn_kernel(tab_ref, qt_ref, kc_ref, vct_ref, ks_ref, vst_ref, kw_ref, vwt_ref, gate_ref, o_ref,
                     bn_ref, edge_ref, cb_ref, vt_ref, neg_ref, add_ref, m_ref, acc_ref, mw_ref, accw_ref,
                     s0_ref, s1_ref, m0_ref, m1_ref, *, n_sel):
    tq = NSA_TQ
    tkf = NSA_TKF
    nq = NSA_QPG
    dh = HEAD_DIM
    w = nq * tq
    g = pl.program_id(0)
    iq = pl.program_id(2)
    t0 = iq * tq
    n_cmp = kc_ref.shape[2]
    n_blk = vt_ref.shape[0]
    bpt = tq // SEL_BLOCK
    far_bucket = REL_BUCKETS - 1

    def head_cols(fn):
        return jnp.concatenate([fn(h) for h in range(nq)], axis=1)

    far_row = head_cols(lambda h: jnp.full((1, tq), tab_ref[far_bucket, g * nq + h] * LOG2E, F32))

    @pl.when((iq == 0) & (pl.program_id(1) == 0))
    def _init():
        jj = lax.broadcasted_iota(jnp.int32, (tq, tq), 0)
        ii = lax.broadcasted_iota(jnp.int32, (tq, tq), 1)
        d = ii - jj
        b_diag = _t5_bucket(d)
        b_prev = _t5_bucket(d + tq)
        mm = lax.broadcasted_iota(jnp.int32, (NSA_CB_ROWS, tq), 0)
        i2 = lax.broadcasted_iota(jnp.int32, (NSA_CB_ROWS, tq), 1)
        dn = i2 - CMP_STRIDE * mm + (9 * CMP_STRIDE - CMP_LEN + 1)
        b_near = _t5_bucket(dn)
        for h in range(nq):
            col = g * nq + h
            far = tab_ref[far_bucket, col] * LOG2E
            cols = slice(h * tq, (h + 1) * tq)
            bn_ref[0:tq, cols] = _bias_lookup(b_prev, tab_ref, col) * LOG2E
            bn_ref[tq:2 * tq, cols] = jnp.where(d >= 0, _bias_lookup(b_diag, tab_ref, col) * LOG2E, NEG_INF)
            edge_ref[:, cols] = jnp.where(jj > ii, far, NEG_INF)
            near = jnp.where(dn >= 0, _bias_lookup(b_near, tab_ref, col) * LOG2E, NEG_INF)
            cbv = jnp.where(mm < NSA_NEAR, near,
                            jnp.where(mm == NSA_NEAR, far, jnp.where(mm == NSA_NEAR + 1, NEG_INF, 0.0)))
            hi, mid, lo = _split3(cbv)
            cb_ref[0, :, cols] = hi
            cb_ref[1, :, cols] = mid
            cb_ref[2, :, cols] = lo

    qt = (head_cols(lambda h: qt_ref[0, h]).astype(F32) * (dh ** -0.5 * LOG2E)).astype(BF16)

    neg_parts = [float(v) for v in _split3_const(NEG_INF)]
    spare = lax.broadcasted_iota(jnp.int32, (16, 1), 0)
    pad_rows = (jnp.where(spare == 0, neg_parts[0], jnp.where(spare == 1, neg_parts[1],
                                                             jnp.where(spare == 2, neg_parts[2], 0.0)))
                + jnp.zeros((16, w), F32)).astype(BF16)

    def aug_rhs(pieces):
        return jnp.concatenate([qt] + list(pieces) + [pad_rows], axis=0)

    zero16 = jnp.zeros((16, w), BF16)
    rhs_plain = aug_rhs([zero16, zero16, zero16])

    def kv_start(key0):
        return pl.multiple_of(key0 + tq, LANE)

    def flash_first(m_st, acc_st, s, vt):
        m = jnp.max(s, axis=0, keepdims=True)
        m_st[...] = m
        acc_st[...] = _dot(vt, jnp.exp2(s - m).astype(BF16))

    def flash_next(m_st, acc_st, s, s_max, vt):
        m_old = m_st[...]
        m_new = jnp.maximum(m_old, s_max)
        acc_st[...] = jnp.exp2(m_old - m_new) * acc_st[...] + _dot(vt, jnp.exp2(s - m_new).astype(BF16))
        m_st[...] = m_new

    def flash_out(acc_st):
        acc = acc_st[...]
        return acc[0:dh] * (1.0 / acc[dh:dh + 1])

    n_lo = t0 // CMP_STRIDE - 9
    ni = lax.broadcasted_iota(jnp.int32, (n_cmp, NSA_CB_ROWS), 0)
    mi = lax.broadcasted_iota(jnp.int32, (n_cmp, NSA_CB_ROWS), 1)
    place = (((mi < NSA_NEAR) & (ni == mi + n_lo)) | ((mi == NSA_NEAR) & (ni < n_lo))
             | ((mi == NSA_NEAR + 1) & (ni >= n_lo + NSA_NEAR)))
    shift = jnp.where(place, 1.0, 0.0).astype(BF16)
    lhs_c = jnp.concatenate([kc_ref[0, 0], shift, shift, shift], axis=1)
    rhs_c = jnp.concatenate([qt, cb_ref[0], cb_ref[1], cb_ref[2]], axis=0)
    lcm = _dot(lhs_c, rhs_c)
    ec = jnp.exp2(lcm - jnp.max(lcm, axis=0, keepdims=True))
    sc = jnp.sum(ec, axis=0, keepdims=True)
    tl = t0 + lax.broadcasted_iota(jnp.int32, (1, w), 1) % tq
    pct = ec * jnp.where(tl >= CMP_LEN - 1, 1.0 / sc, 0.0)
    o_ct = _dot(vct_ref[0, 0], pct.astype(BF16))

    psum = pct[:, 0:tq]
    for h in range(1, nq):
        psum = psum + pct[:, h * tq:(h + 1) * tq]
    p_hi = psum.astype(BF16)
    p_lo = (psum - p_hi.astype(F32)).astype(BF16)
    jb = lax.broadcasted_iota(jnp.int32, (n_blk, n_cmp), 0)
    nb = lax.broadcasted_iota(jnp.int32, (n_blk, n_cmp), 1)
    ovl = jnp.where((CMP_STRIDE * nb < SEL_BLOCK * (jb + 1)) & (CMP_STRIDE * nb + CMP_LEN > SEL_BLOCK * jb),
                    1.0, 0.0).astype(BF16)
    imp_t = _dot(ovl, p_hi) + _dot(ovl, p_lo)
    jb2 = lax.broadcasted_iota(jnp.int32, (n_blk, tq), 0)
    cur = (t0 + lax.broadcasted_iota(jnp.int32, (n_blk, tq), 1)) // SEL_BLOCK
    forced = (jb2 == 0) | (jb2 == cur) | (jb2 == cur - 1)
    val = jnp.where(jb2 > cur, -1e9, jnp.where(forced, 1e9, imp_t))
    vt_ref[...] = val

    def rank_body(jp, rank):
        row = vt_ref[pl.ds(jp, 1), :]
        beats = (row > val) | ((row == val) & (jp < jb2))
        return rank + jnp.where(beats, 1, 0)

    n_causal = (t0 + tq) // SEL_BLOCK
    rank = lax.fori_loop(0, jnp.where(n_causal <= n_sel, 0, n_causal), rank_body,
                         jnp.zeros((n_blk, tq), jnp.int32))
    neg_t = jnp.where(rank < n_sel, 0.0, NEG_INF)
    neg4 = jnp.concatenate([neg_t] * nq, axis=1)
    neg_ref[...] = neg4

    blk0 = iq * bpt
    n_far = jnp.maximum(blk0 - bpt, 0)

    def split_rows(masked, live_row):
        return [jnp.where(masked, neg_parts[i], piece.astype(F32)) for i, piece in enumerate(_split3(live_row))]

    brow = lax.broadcasted_iota(jnp.int32, (n_blk, 1), 0)
    far_parts = split_rows((neg4 < 0.0) | (brow >= n_far), far_row)
    for part in range(3):
        add_ref[part] = far_parts[part].reshape(n_blk // 16, 16, w).astype(BF16)

    near_rows = []
    for j in range(16):
        off = (j - (blk0 - bpt)) % 16
        row = neg_ref[pl.ds(jnp.clip(blk0 - bpt + off, 0, n_blk - 1), 1), :]
        near_rows.append(jnp.where(off < 2 * bpt, row, 0.0))
    near_mask = jnp.concatenate(near_rows, axis=0) < 0.0
    rhs_near = aug_rhs([p.astype(BF16) for p in split_rows(near_mask, jnp.zeros((1, w), F32))])

    near0 = kv_start(t0 - tq)
    edge0 = kv_start(jnp.maximum(t0 - 2 * tq, -tq))
    s_win = _dot(kw_ref[0, 0, pl.ds(near0, 2 * tq), :], rhs_plain) + bn_ref[...]
    s_edge = _dot(kw_ref[0, 0, pl.ds(edge0, tq), :], rhs_plain) + edge_ref[...]
    flash_first(mw_ref, accw_ref, s_win, vwt_ref[0, 0, :, pl.ds(near0, 2 * tq)])
    flash_next(mw_ref, accw_ref, s_edge, jnp.max(s_edge, axis=0, keepdims=True), vwt_ref[0, 0, :, pl.ds(edge0, tq)])
    o_wt = flash_out(accw_ref)

    s_sel = _dot(ks_ref[0, 0, pl.ds(near0, 2 * tq), :], rhs_near) + bn_ref[...]
    flash_first(m_ref, acc_ref, s_sel, vst_ref[0, 0, :, pl.ds(near0, 2 * tq)])

    bpf = tkf // SEL_BLOCK
    n_tiles = n_far // bpf

    def qk_stage(c, s_buf, m_buf):
        chunk = (c * bpf) // 16
        s = _dot(ks_ref[0, 0, pl.ds(kv_start(c * tkf), tkf), :],
                 aug_rhs([add_ref[0, chunk], add_ref[1, chunk], add_ref[2, chunk]]))
        s_buf[...] = s
        m_buf[...] = jnp.max(s, axis=0, keepdims=True)

    def sm_stage(c, s_buf, m_buf):
        flash_next(m_ref, acc_ref, s_buf[...], m_buf[...], vst_ref[0, 0, :, pl.ds(kv_start(c * tkf), tkf)])

    @pl.when(n_tiles > 0)
    def _():
        qk_stage(0, s0_ref, m0_ref)

    def far_body(i, carry):
        qk_stage(2 * i + 1, s1_ref, m1_ref)
        sm_stage(2 * i, s0_ref, m0_ref)
        qk_stage(2 * i + 2, s0_ref, m0_ref)
        sm_stage(2 * i + 1, s1_ref, m1_ref)
        return carry

    lax.fori_loop(0, (n_tiles + 1) // 2, far_body, 0)
    o_st = flash_out(acc_ref)

    sg = jax.nn.sigmoid(gate_ref[0, 0])
    outs = []
    for h in range(nq):
        cols = slice(h * tq, (h + 1) * tq)
        outs.append(sg[3 * h:3 * h + 1] * o_ct[:, cols] + sg[3 * h + 1:3 * h + 2] * o_st[:, cols]
                    + sg[3 * h + 2:3 * h + 3] * o_wt[:, cols])
    o_ref[0] = jnp.concatenate(outs, axis=0).T.astype(o_ref.dtype)


def _nsa_attention(tab, qt, kc, vct, ks, vst, kw, vwt, gate):
    b, _, dh, t = qt.shape
    g = NSA_KV_HEADS
    nq = NSA_QPG
    tq = NSA_TQ
    w = nq * tq
    assert t % tq == 0 and tq >= REL_MAX_DIST and NSA_WINDOW == 2 * tq and tq % SEL_BLOCK == 0
    assert NSA_TKF % SEL_BLOCK == 0 and t >= NSA_TKF and tq % LANE == 0
    assert NSA_NEAR + 1 < NSA_CB_ROWS
    n_cmp = kc.shape[2]
    n_blk = t // SEL_BLOCK
    n_sel = min(SEL_COUNT, n_blk)
    assert n_sel >= 3
    tp = ks.shape[2]
    bpf = NSA_TKF // SEL_BLOCK
    assert tp == t + tq and vst.shape[3] == tp and 16 % bpf == 0 and n_blk % 16 == 0 and tq % NSA_TKF == 0
    assert ks.shape[3] == 2 * dh and vst.shape[2] == dh + 16
    kspec = pl.BlockSpec((1, 1, tp, 2 * dh), lambda j, i, k: (i, j, 0, 0))
    vtspec = pl.BlockSpec((1, 1, dh + 16, tp), lambda j, i, k: (i, j, 0, 0))
    return pl.pallas_call(
        functools.partial(_nsa_attn_kernel, n_sel=n_sel),
        grid=(g, b, t // tq),
        in_specs=[pl.BlockSpec(memory_space=pltpu.SMEM),
                  pl.BlockSpec((1, nq, dh, tq), lambda j, i, k: (i, j, 0, k)),
                  pl.BlockSpec((1, 1, n_cmp, dh), lambda j, i, k: (i, j, 0, 0)),
                  pl.BlockSpec((1, 1, dh, n_cmp), lambda j, i, k: (i, j, 0, 0)),
                  kspec, vtspec, kspec, vtspec,
                  pl.BlockSpec((1, 1, 16, tq), lambda j, i, k: (i, j, 0, k))],
        out_specs=pl.BlockSpec((1, tq, nq * dh), lambda j, i, k: (i, k, j)),
        out_shape=jax.ShapeDtypeStruct((b, t, g * nq * dh), BF16),
        scratch_shapes=[pltpu.VMEM((2 * tq, w), F32),
                        pltpu.VMEM((tq, w), F32),
                        pltpu.VMEM((3, NSA_CB_ROWS, w), BF16),
                        pltpu.VMEM((n_blk, tq), F32),
                        pltpu.VMEM((n_blk, w), F32),
                        pltpu.VMEM((3, n_blk // 16, 16, w), BF16),
                        pltpu.VMEM((1, w), F32),
                        pltpu.VMEM((dh + 16, w), F32),
                        pltpu.VMEM((1, w), F32),
                        pltpu.VMEM((dh + 16, w), F32),
                        pltpu.VMEM((NSA_TKF, w), F32), pltpu.VMEM((NSA_TKF, w), F32),
                        pltpu.VMEM((1, w), F32), pltpu.VMEM((1, w), F32)],
        compiler_params=_cparams(("arbitrary", "arbitrary", "arbitrary")),
        name="nsa_attn",
    )(tab, qt, kc, vct, ks, vst, kw, vwt, gate)


def _nsa_layer(x, w_in, w_out, cmp_pos, cmp_w1, cmp_w2, rel_bias, gain, bias, alpha):
    b, t, d = x.shape
    g, nq, dh = NSA_KV_HEADS, NSA_QPG, HEAD_DIM
    kvw = g * dh
    x2 = x.reshape(b * t, d)
    off = ATTN_HEADS * dh
    cols_t = np.concatenate([np.arange(off), np.arange(off + 3 * kvw, off + 4 * kvw),
                             np.arange(off + 5 * kvw, off + 6 * kvw)])
    cols_k = np.concatenate([np.arange(off + 2 * kvw, off + 3 * kvw), np.arange(off + 4 * kvw, off + 5 * kvw)])
    cols_f = np.concatenate([np.arange(off, off + 2 * kvw), np.arange(off + 6 * kvw, off + 6 * kvw + 3 * ATTN_HEADS)])
    n_pad = -len(cols_f) % LANE
    w_f = jnp.pad(w_in[:, cols_f], ((0, 0), (0, n_pad))).astype(BF16)
    ht = _proj_t(x, w_in[:, cols_t].T.astype(BF16), tn=512)
    hk = _proj(x2, w_in[:, cols_k].astype(BF16), tn=len(cols_k), out_dtype=BF16).reshape(b, t, len(cols_k))
    hf = _proj(x2, w_f, tn=len(cols_f) + n_pad).reshape(b, t, len(cols_f) + n_pad)

    def heads(src, lo, n_heads):
        return src[:, :, lo:lo + n_heads * dh].reshape(b, t, n_heads, dh).transpose(0, 2, 1, 3)

    def heads_t(lo, n_heads):
        return ht[:, lo:lo + n_heads * dh, :].reshape(b, n_heads, dh, t)

    qt = heads_t(0, ATTN_HEADS)
    zk = heads(hf, 0, g).reshape(b, g, t // CMP_STRIDE, CMP_STRIDE * dh)
    zv = heads(hf, kvw, g).reshape(b, g, t // CMP_STRIDE, CMP_STRIDE * dh)
    tp = t + NSA_TQ
    is_pad = np.arange(tp) < NSA_TQ
    slot = (np.arange(tp) - NSA_TQ) // SEL_BLOCK % 16
    onehot = ((np.arange(16)[None, :] == slot[:, None]) & ~is_pad[:, None]).astype(np.float32)
    pad_cols = ((np.arange(16)[None, :] < 3) & is_pad[:, None]).astype(np.float32)
    k_cols = jnp.asarray(np.concatenate([onehot] * 3 + [pad_cols], axis=1), BF16)
    v_rows = jnp.asarray((np.arange(16)[:, None] == 0) * np.ones((1, tp)), BF16)

    def aug_k(z):
        z = jnp.pad(z, ((0, 0), (0, 0), (NSA_TQ, 0), (0, 0)))
        return jnp.concatenate([z, jnp.broadcast_to(k_cols, (b, g, tp, dh))], axis=3)

    def aug_vt(z):
        z = jnp.pad(z, ((0, 0), (0, 0), (0, 0), (NSA_TQ, 0)))
        return jnp.concatenate([z, jnp.broadcast_to(v_rows, (b, g, 16, tp))], axis=2)

    ks = aug_k(heads(hk, 0, g))
    vst = aug_vt(heads_t(off, g))
    kw = aug_k(heads(hk, kvw, g))
    vwt = aug_vt(heads_t(off + kvw, g))
    gate = hf[:, :, 2 * kvw:2 * kvw + 3 * ATTN_HEADS].reshape(b, t, g, 3 * nq)
    gate = jnp.pad(gate.transpose(0, 2, 3, 1), ((0, 0), (0, 0), (0, 16 - 3 * nq), (0, 0)))

    pos = cmp_pos.reshape(2, 2, 1, CMP_STRIDE * dh)
    w1 = cmp_w1.reshape(2, 2, CMP_STRIDE * dh, cmp_w1.shape[-1]).astype(BF16)
    kc, vct = _nsa_compress(zk, zv, pos, w1, cmp_w2.astype(BF16))
    o = _nsa_attention(rel_bias, qt, kc, vct, ks, vst, kw, vwt, gate)
    y = _proj_ln(o.reshape(b * t, d), w_out.astype(BF16), x2, gain, bias, alpha)
    return y.reshape(b, t, d)


def _hgrn_kernel(lbp_ref, gain_ref, zq_ref, zf_ref, zi_ref, zg_ref, o_ref, st_ref, *, layer):
    C = HGRN_TC
    t = zq_ref.shape[1]
    kdim = zq_ref.shape[2]

    p = lbp_ref[...]
    e = jnp.exp(p - jnp.max(p, axis=0, keepdims=True))
    sm = e / jnp.sum(e, axis=0, keepdims=True)
    cs = sm[0:1]
    for r in range(1, layer + 1):
        cs = cs + sm[r:r + 1]
    lb = cs - sm[0:1]
    log_lb = jnp.log(lb)
    log_1m = jnp.log1p(-lb)
    gain = gain_ref[...]

    st_ref[...] = jnp.zeros(st_ref.shape, F32)

    rr = lax.broadcasted_iota(jnp.int32, (C, C), 0)
    cc = lax.broadcasted_iota(jnp.int32, (C, C), 1)
    srow = lax.broadcasted_iota(jnp.int32, (C, 1), 0)
    halves = [1 << i for i in range(C.bit_length() - 1)]
    small = [h for h in halves if h < 8]
    sel_rows = [jnp.where(cc <= rr, 1.0, 0.0).astype(BF16)]
    sel_rows += [jnp.where(cc <= (rr // (2 * h)) * (2 * h) + h - 1, 1.0, 0.0).astype(BF16) for h in small]
    cum_sel = jnp.concatenate(sel_rows, axis=0)
    right = [(srow // h) % 2 == 1 for h in halves]
    same_blk = [rr // (2 * h) == cc // (2 * h) for h in halves]
    eye = rr == cc

    def chunk(c, carry):
        r0 = pl.multiple_of(c * C, C)
        zq = zq_ref[0, pl.ds(r0, C), :]
        zf = zf_ref[0, pl.ds(r0, C), :]
        v = zi_ref[0, pl.ds(r0, C), :]
        zg = zg_ref[0, pl.ds(r0, C), :]
        q = jax.nn.silu(zq)
        log_f = jnp.logaddexp(log_lb, log_1m + jax.nn.log_sigmoid(zf))
        kk = (1.0 - lb) * jax.nn.sigmoid(-zf)
        vb = v.astype(BF16)

        g_hi, g_mid, g_lo = _split3(log_f)
        cums = _dot(cum_sel, g_hi) + _dot(cum_sel, g_mid) + _dot(cum_sel, g_lo)
        bcum = cums[0:C]
        b_last = bcum[C - 1:C, :]

        st = st_ref[...]
        o = _dot_nt((q * jnp.exp(bcum)).astype(BF16), st.astype(BF16))

        a = jnp.where(eye, _dot_nt(q.astype(BF16), kk.astype(BF16)), 0.0)
        for lvl, h in enumerate(halves):
            if h in small:
                b_ref = cums[(lvl + 1) * C:(lvl + 2) * C]
            else:
                b_ref = jnp.broadcast_to(bcum.reshape(C // (2 * h), 2 * h, kdim)[:, h - 1:h, :],
                                         (C // (2 * h), 2 * h, kdim)).reshape(C, kdim)
            e = jnp.exp(-jnp.abs(bcum - b_ref))
            q_side = jnp.where(right[lvl], q * e, 0.0)
            k_side = jnp.where(right[lvl], 0.0, kk * e)
            a = a + jnp.where(same_blk[lvl], _dot_nt(q_side.astype(BF16), k_side.astype(BF16)), 0.0)
        o = o + _dot(a.astype(BF16), vb)

        kd = kk * jnp.exp(b_last - bcum)
        st_ref[...] = st * jnp.exp(b_last) + _dot(v.T.astype(BF16), kd.astype(BF16))

        o = o * lax.rsqrt(jnp.mean(o * o, axis=-1, keepdims=True) + RMS_EPS) * gain
        o = o * jax.nn.silu(zg)
        o_ref[0, pl.ds(r0, C), :] = o.astype(o_ref.dtype)
        return carry

    lax.fori_loop(0, t // C, chunk, 0, unroll=4)


def _hgrn_layer(x, w_in, w_out, norm_gain, lb_param, layer, gain, bias, alpha):
    b, t, d = x.shape
    nh = HGRN_HEADS
    kd = d // nh
    x2 = x.reshape(b * t, d)
    h = _proj(x2, w_in.astype(BF16), tn=1024).reshape(b, t, 4 * d)

    def zspec(part):
        return pl.BlockSpec((1, t, kd), lambda i, j: (i, 0, part * nh + j))

    o = pl.pallas_call(
        functools.partial(_hgrn_kernel, layer=layer),
        grid=(b, nh),
        in_specs=[pl.BlockSpec((lb_param.shape[0], kd), lambda i, j: (0, j)),
                  pl.BlockSpec((1, kd), lambda i, j: (0, 0)),
                  zspec(0), zspec(1), zspec(2), zspec(3)],
        out_specs=pl.BlockSpec((1, t, kd), lambda i, j: (i, 0, j)),
        out_shape=jax.ShapeDtypeStruct((b, t, d), BF16),
        scratch_shapes=[pltpu.VMEM((kd, kd), F32)],
        compiler_params=_cparams(("parallel", "parallel")),
        name="hgrn",
    )(lb_param, norm_gain.reshape(1, kd), h, h, h, h)
    y = _proj_ln(o.reshape(b * t, d), w_out.astype(BF16), x2, gain, bias, alpha)
    return y.reshape(b, t, d)


def _swa_kernel(tab_ref, sink_ref, q_ref, k_ref, v_ref, o_ref, bd_ref, bp_ref):
    L = SWA_WINDOW
    nq = SWA_QPG
    dh = HEAD_DIM
    g = pl.program_id(1)
    n = pl.program_id(2)
    ii = lax.broadcasted_iota(jnp.int32, (L, L), 0)
    jj = lax.broadcasted_iota(jnp.int32, (L, L), 1)

    @pl.when(n == 0)
    def _init():
        b_diag = _t5_bucket(ii - jj)
        b_prev = _t5_bucket(ii - jj + L)
        for h in range(nq):
            bd_ref[h] = _bias_lookup(b_diag, tab_ref, g * nq + h)
            bp_ref[h] = _bias_lookup(b_prev, tab_ref, g * nq + h)

    qs = (q_ref[0].astype(F32) * (dh ** -0.5)).astype(BF16).reshape(nq * L, dh)
    cur = pl.multiple_of(n * L, L)
    prev = pl.multiple_of(jnp.maximum(n - 1, 0) * L, L)
    k_c = k_ref[0, 0, pl.ds(cur, L), :]
    v_c = v_ref[0, 0, pl.ds(cur, L), :]
    k_p = k_ref[0, 0, pl.ds(prev, L), :]
    v_p = v_ref[0, 0, pl.ds(prev, L), :]

    s_c = _dot_nt(qs, k_c).reshape(nq, L, L) + bd_ref[...]
    s_p = _dot_nt(qs, k_p).reshape(nq, L, L) + bp_ref[...]
    s_c = jnp.where((ii >= jj)[None], s_c, NEG_INF)
    s_p = jnp.where(((jj > ii) & (n > 0))[None], s_p, NEG_INF)
    sink = _per_head_scalar(nq, lambda h: sink_ref[g * nq + h])
    m = jnp.maximum(jnp.maximum(jnp.max(s_c, axis=-1, keepdims=True), jnp.max(s_p, axis=-1, keepdims=True)), sink)
    e_c = jnp.exp(s_c - m)
    e_p = jnp.exp(s_p - m)
    den = jnp.sum(e_c, axis=-1, keepdims=True) + jnp.sum(e_p, axis=-1, keepdims=True) + jnp.exp(sink - m)
    pv = (_dot(e_c.reshape(nq * L, L).astype(BF16), v_c) + _dot(e_p.reshape(nq * L, L).astype(BF16), v_p))
    o = pv.reshape(nq, L, dh) / den
    o_ref[0] = jnp.concatenate([o[h] for h in range(nq)], axis=1).astype(o_ref.dtype)


def _swa_layer(x, w_in, w_out, sinks, rel_bias, gain, bias, alpha):
    b, t, d = x.shape
    kvh, nq, dh, L = SWA_KV_HEADS, SWA_QPG, HEAD_DIM, SWA_WINDOW
    assert t % L == 0 and L >= REL_MAX_DIST
    x2 = x.reshape(b * t, d)
    n_in = w_in.shape[1]
    h = _proj(x2, w_in.astype(BF16), tn=n_in // 2, out_dtype=BF16).reshape(b, t, n_in)

    def heads(lo, n_heads):
        return h[:, :, lo:lo + n_heads * dh].reshape(b, t, n_heads, dh).transpose(0, 2, 1, 3)

    q = heads(0, ATTN_HEADS)
    k = heads(ATTN_HEADS * dh, kvh)
    v = heads(ATTN_HEADS * dh + kvh * dh, kvh)
    kvspec = pl.BlockSpec((1, 1, t, dh), lambda i, j, n: (i, j, 0, 0))
    o = pl.pallas_call(
        _swa_kernel,
        grid=(b, kvh, t // L),
        in_specs=[pl.BlockSpec(memory_space=pltpu.SMEM),
                  pl.BlockSpec(memory_space=pltpu.SMEM),
                  pl.BlockSpec((1, nq, L, dh), lambda i, j, n: (i, j, n, 0)),
                  kvspec, kvspec],
        out_specs=pl.BlockSpec((1, L, nq * dh), lambda i, j, n: (i, n, j)),
        out_shape=jax.ShapeDtypeStruct((b, t, d), BF16),
        scratch_shapes=[pltpu.VMEM((nq, L, L), F32), pltpu.VMEM((nq, L, L), F32)],
        compiler_params=_cparams(("parallel", "parallel", "arbitrary")),
        name="swa_attn",
    )(rel_bias, sinks, q, k, v)
    y = _proj_ln(o.reshape(b * t, d), w_out.astype(BF16), x2, gain, bias, alpha)
    return y.reshape(b, t, d)


def kernel(x, rel_bias, ln_gain, ln_bias, ffn1_w_gate, ffn1_w_up, ffn1_w_down, ffn2_w_gate, ffn2_w_up,
           ffn2_w_down, nsa_w_in, nsa_w_out, nsa_cmp_pos, nsa_cmp_w1, nsa_cmp_w2, hgrn_w_in, hgrn_w_out,
           hgrn_norm_gain, hgrn_lb, swa_w_in, swa_w_out, swa_sinks):
    depth = ln_gain.shape[0]
    alpha = (2.0 * depth) ** 0.25
    b, t, d = x.shape

    def ffn(x, wg, wu, wd, gain, bias):
        y = _ffn_ln(x.reshape(b * t, d), wg.astype(BF16), wu.astype(BF16), wd.astype(BF16), gain, bias, alpha)
        return y.reshape(b, t, d)

    for i in range(depth):
        x = ffn(x, ffn1_w_gate[i], ffn1_w_up[i], ffn1_w_down[i], ln_gain[i, 0], ln_bias[i, 0])
        kind, slot = i % N_MIXERS, i // N_MIXERS
        if kind == 0:
            x = _nsa_layer(x, nsa_w_in[slot], nsa_w_out[slot], nsa_cmp_pos[slot], nsa_cmp_w1[slot],
                           nsa_cmp_w2[slot], rel_bias, ln_gain[i, 1], ln_bias[i, 1], alpha)
        elif kind == 1:
            x = _hgrn_layer(x, hgrn_w_in[slot], hgrn_w_out[slot], hgrn_norm_gain[slot], hgrn_lb, i,
                            ln_gain[i, 1], ln_bias[i, 1], alpha)
        else:
            x = _swa_layer(x, swa_w_in[slot], swa_w_out[slot], swa_sinks[slot], rel_bias,
                           ln_gain[i, 1], ln_bias[i, 1], alpha)
        x = ffn(x, ffn2_w_gate[i], ffn2_w_up[i], ffn2_w_down[i], ln_gain[i, 2], ln_bias[i, 2])
    return x
```

```python
import functools
import math

import jax
import jax.numpy as jnp
import numpy as np
from jax import lax
from jax.experimental import pallas as pl
from jax.experimental.pallas import tpu as pltpu

F32 = jnp.float32
BF16 = jnp.bfloat16

DEPTH = 4
N_MIXERS = 3
REL_BUCKETS = 32
REL_EXACT = REL_BUCKETS // 2
REL_MAX_DIST = 128
ATTN_HEADS = 16
HEAD_DIM = 64
NSA_KV_HEADS = 4
NSA_QPG = ATTN_HEADS // NSA_KV_HEADS
CMP_STRIDE = 16
CMP_LEN = 2 * CMP_STRIDE
SEL_BLOCK = 64
SEL_COUNT = 16
NSA_WINDOW = 512
HGRN_HEADS = 8
HGRN_CHUNK = 64
HGRN_TC = 128
SWA_KV_HEADS = 2
SWA_QPG = ATTN_HEADS // SWA_KV_HEADS
SWA_WINDOW = 128
LN_EPS = 1e-5
RMS_EPS = 1e-6
NEG_INF = -1e30
LOG2E = 1.4426950408889634

LANE = 128
VMEM_LIMIT = 48 * 1024 * 1024
NSA_TQ = 256
NSA_TKF = 256
NSA_NEAR = NSA_TQ // CMP_STRIDE + 8
NSA_CB_ROWS = 32


def _cparams(sem):
    return pltpu.CompilerParams(dimension_semantics=sem, vmem_limit_bytes=VMEM_LIMIT)


def _dot(a, b):
    return jnp.dot(a, b, preferred_element_type=F32)


def _dot_nt(a, b):
    return lax.dot_general(a, b, (((1,), (1,)), ((), ())), preferred_element_type=F32)


def _layer_norm_rows(y, g, b):
    mu = jnp.mean(y, axis=-1, keepdims=True)
    yc = y - mu
    var = jnp.mean(yc * yc, axis=-1, keepdims=True)
    return yc * lax.rsqrt(var + LN_EPS) * g + b


def _ffn_ln_kernel(x_ref, wg_ref, wu_ref, wd_ref, g_ref, b_ref, o_ref, xb_ref, acc_ref, *, alpha):
    j = pl.program_id(1)

    @pl.when(j == 0)
    def _():
        xb_ref[...] = x_ref[...].astype(BF16)
        acc_ref[...] = jnp.zeros_like(acc_ref)

    xb = xb_ref[...]
    gate = _dot(xb, wg_ref[...])
    up = _dot(xb, wu_ref[...])
    h = (jax.nn.silu(gate) * up).astype(BF16)
    acc_ref[...] += _dot(h, wd_ref[...])

    @pl.when(j == pl.num_programs(1) - 1)
    def _():
        y = alpha * x_ref[...] + 0.5 * acc_ref[...]
        o_ref[...] = _layer_norm_rows(y, g_ref[...], b_ref[...])


def _ffn_ln(x, wg, wu, wd, gain, bias, alpha, tm=512, tf=1408):
    n, d = x.shape
    f = wg.shape[1]
    tm = min(tm, n)
    assert n % tm == 0 and f % tf == 0
    return pl.pallas_call(
        functools.partial(_ffn_ln_kernel, alpha=alpha),
        grid=(n // tm, f // tf),
        in_specs=[
            pl.BlockSpec((tm, d), lambda i, j: (i, 0)),
            pl.BlockSpec((d, tf), lambda i, j: (0, j)),
            pl.BlockSpec((d, tf), lambda i, j: (0, j)),
            pl.BlockSpec((tf, d), lambda i, j: (j, 0)),
            pl.BlockSpec((1, d), lambda i, j: (0, 0)),
            pl.BlockSpec((1, d), lambda i, j: (0, 0)),
        ],
        out_specs=pl.BlockSpec((tm, d), lambda i, j: (i, 0)),
        out_shape=jax.ShapeDtypeStruct((n, d), F32),
        scratch_shapes=[pltpu.VMEM((tm, d), BF16), pltpu.VMEM((tm, d), F32)],
        compiler_params=_cparams(("parallel", "arbitrary")),
        name="ffn_ln",
    )(x, wg, wu, wd, gain.reshape(1, d), bias.reshape(1, d))


def _proj_kernel(x_ref, w_ref, o_ref, xb_ref):
    @pl.when(pl.program_id(1) == 0)
    def _():
        xb_ref[...] = x_ref[...].astype(BF16)

    o_ref[...] = _dot(xb_ref[...], w_ref[...]).astype(o_ref.dtype)


def _proj(x, w, tn, tm=1024, out_dtype=F32):
    n, k = x.shape
    m = w.shape[1]
    tm = min(tm, n)
    assert n % tm == 0 and m % tn == 0
    return pl.pallas_call(
        _proj_kernel,
        grid=(n // tm, m // tn),
        in_specs=[pl.BlockSpec((tm, k), lambda i, j: (i, 0)),
                  pl.BlockSpec((k, tn), lambda i, j: (0, j))],
        out_specs=pl.BlockSpec((tm, tn), lambda i, j: (i, j)),
        out_shape=jax.ShapeDtypeStruct((n, m), out_dtype),
        scratch_shapes=[pltpu.VMEM((tm, k), BF16)],
        compiler_params=_cparams(("parallel", "arbitrary")),
        name="proj",
    )(x, w)


def _proj_t_kernel(x_ref, wt_ref, o_ref, xb_ref):
    @pl.when(pl.program_id(2) == 0)
    def _():
        xb_ref[...] = x_ref[0].astype(BF16)

    o_ref[0] = _dot_nt(wt_ref[...], xb_ref[...]).astype(o_ref.dtype)


def _proj_t(x, wt, tn, tm=1024, out_dtype=BF16):
    b, t, k = x.shape
    m = wt.shape[0]
    tm = min(tm, t)
    assert t % tm == 0 and m % tn == 0
    return pl.pallas_call(
        _proj_t_kernel,
        grid=(b, t // tm, m // tn),
        in_specs=[pl.BlockSpec((1, tm, k), lambda i, j, n: (i, j, 0)),
                  pl.BlockSpec((tn, k), lambda i, j, n: (n, 0))],
        out_specs=pl.BlockSpec((1, tn, tm), lambda i, j, n: (i, n, j)),
        out_shape=jax.ShapeDtypeStruct((b, m, t), out_dtype),
        scratch_shapes=[pltpu.VMEM((tm, k), BF16)],
        compiler_params=_cparams(("parallel", "parallel", "arbitrary")),
        name="proj_t",
    )(x, wt)


def _proj_ln_kernel(a_ref, w_ref, r_ref, g_ref, b_ref, o_ref, *, alpha):
    y = _dot(a_ref[...].astype(BF16), w_ref[...])
    o_ref[...] = _layer_norm_rows(alpha * r_ref[...] + y, g_ref[...], b_ref[...])


def _proj_ln(a, w, res, gain, bias, alpha, tm=512):
    n, k = a.shape
    d = w.shape[1]
    tm = min(tm, n)
    assert n % tm == 0
    return pl.pallas_call(
        functools.partial(_proj_ln_kernel, alpha=alpha),
        grid=(n // tm,),
        in_specs=[pl.BlockSpec((tm, k), lambda i: (i, 0)),
                  pl.BlockSpec((k, d), lambda i: (0, 0)),
                  pl.BlockSpec((tm, d), lambda i: (i, 0)),
                  pl.BlockSpec((1, d), lambda i: (0, 0)),
                  pl.BlockSpec((1, d), lambda i: (0, 0))],
        out_specs=pl.BlockSpec((tm, d), lambda i: (i, 0)),
        out_shape=jax.ShapeDtypeStruct((n, d), F32),
        compiler_params=_cparams(("parallel",)),
        name="proj_ln",
    )(a, w, res, gain.reshape(1, d), bias.reshape(1, d))


def _t5_bucket(dist):
    n = jnp.maximum(dist, 0)
    nf = jnp.maximum(n, 1).astype(F32)
    large = REL_EXACT + (jnp.log(nf / REL_EXACT) / math.log(REL_MAX_DIST / REL_EXACT)
                         * (REL_BUCKETS - REL_EXACT)).astype(jnp.int32)
    return jnp.where(n < REL_EXACT, n, jnp.minimum(large, REL_BUCKETS - 1))


def _bias_lookup(bucket, tab_ref, col):
    out = jnp.zeros(bucket.shape, F32)
    for k in range(REL_BUCKETS):
        out = jnp.where(bucket == k, tab_ref[k, col], out)
    return out


def _per_head_scalar(n_heads, fn):
    hidx = lax.broadcasted_iota(jnp.int32, (n_heads, 1, 1), 0)
    out = jnp.zeros((n_heads, 1, 1), F32)
    for h in range(n_heads):
        out = jnp.where(hidx == h, fn(h), out)
    return out


def _split3_const(value):
    x = np.float32(value)
    hi = np.float32(np.asarray(x, dtype=jnp.bfloat16))
    mid = np.float32(np.asarray(x - hi, dtype=jnp.bfloat16))
    lo = np.float32(np.asarray(x - hi - mid, dtype=jnp.bfloat16))
    return hi, mid, lo


def _split3(x):
    hi = x.astype(BF16)
    r1 = x - hi.astype(F32)
    mid = r1.astype(BF16)
    lo = (r1 - mid.astype(F32)).astype(BF16)
    return hi, mid, lo


def _nsa_compress_kernel(zk_ref, zv_ref, pos_ref, w1_ref, w2_ref, w2t_ref, kc_ref, vct_ref):
    def hidden(idx, z_ref):
        z = z_ref[0, 0]
        u = _dot((z + pos_ref[idx, 0]).astype(BF16), w1_ref[idx, 0])
        v = _dot((z + pos_ref[idx, 1]).astype(BF16), w1_ref[idx, 1])
        return jax.nn.gelu(u + pltpu.roll(v, z.shape[0] - 1, 0)).astype(BF16)

    kc_ref[0, 0] = _dot(hidden(0, zk_ref), w2_ref[0]).astype(kc_ref.dtype)
    vct_ref[0, 0] = _dot_nt(w2t_ref[1], hidden(1, zv_ref)).astype(vct_ref.dtype)


def _nsa_compress(zk, zv, pos, w1, w2):
    b, g, nc, kd = zk.shape
    hid = w1.shape[-1]
    dh = w2.shape[-1]
    zspec = pl.BlockSpec((1, 1, nc, kd), lambda i, j: (i, j, 0, 0))
    return pl.pallas_call(
        _nsa_compress_kernel,
        grid=(b, g),
        in_specs=[zspec, zspec,
                  pl.BlockSpec((2, 2, 1, kd), lambda i, j: (0, 0, 0, 0)),
                  pl.BlockSpec((2, 2, kd, hid), lambda i, j: (0, 0, 0, 0)),
                  pl.BlockSpec((2, hid, dh), lambda i, j: (0, 0, 0)),
                  pl.BlockSpec((2, dh, hid), lambda i, j: (0, 0, 0))],
        out_specs=[pl.BlockSpec((1, 1, nc, dh), lambda i, j: (i, j, 0, 0)),
                   pl.BlockSpec((1, 1, dh, nc), lambda i, j: (i, j, 0, 0))],
        out_shape=[jax.ShapeDtypeStruct((b, g, nc, dh), BF16), jax.ShapeDtypeStruct((b, g, dh, nc), BF16)],
        compiler_params=_cparams(("parallel", "parallel")),
        name="nsa_compress",
    )(zk, zv, pos, w1, w2, w2.transpose(0, 2, 1))


def _nsa_attn_kernel(tab_ref, qt_ref, kc_ref, vct_ref, ks_ref, vst_ref, kw_ref, vwt_ref, gate_ref, o_ref,
                     bn_ref, edge_ref, cb_ref, neg_ref, add_ref, m_ref, acc_ref, mw_ref, accw_ref,
                     s0_ref, s1_ref, m0_ref, m1_ref, *, n_sel):
    tq = NSA_TQ
    tkf = NSA_TKF
    nq = NSA_QPG
    dh = HEAD_DIM
    w = nq * tq
    g = pl.program_id(0)
    iq = pl.program_id(2)
    t0 = iq * tq
    n_cmp = kc_ref.shape[2]
    n_blk = neg_ref.shape[0]
    bpt = tq // SEL_BLOCK
    far_bucket = REL_BUCKETS - 1

    def head_cols(fn):
        return jnp.concatenate([fn(h) for h in range(nq)], axis=1)

    far_row = head_cols(lambda h: jnp.full((1, tq), tab_ref[far_bucket, g * nq + h] * LOG2E, F32))

    @pl.when((iq == 0) & (pl.program_id(1) == 0))
    def _init():
        jj = lax.broadcasted_iota(jnp.int32, (tq, tq), 0)
        ii = lax.broadcasted_iota(jnp.int32, (tq, tq), 1)
        d = ii - jj
        b_diag = _t5_bucket(d)
        b_prev = _t5_bucket(d + tq)
        mm = lax.broadcasted_iota(jnp.int32, (NSA_CB_ROWS, tq), 0)
        i2 = lax.broadcasted_iota(jnp.int32, (NSA_CB_ROWS, tq), 1)
        dn = i2 - CMP_STRIDE * mm + (9 * CMP_STRIDE - CMP_LEN + 1)
        b_near = _t5_bucket(dn)
        for h in range(nq):
            col = g * nq + h
            far = tab_ref[far_bucket, col] * LOG2E
            cols = slice(h * tq, (h + 1) * tq)
            bn_ref[0:tq, cols] = _bias_lookup(b_prev, tab_ref, col) * LOG2E
            bn_ref[tq:2 * tq, cols] = jnp.where(d >= 0, _bias_lookup(b_diag, tab_ref, col) * LOG2E, NEG_INF)
            edge_ref[:, cols] = jnp.where(jj > ii, far, NEG_INF)
            near = jnp.where(dn >= 0, _bias_lookup(b_near, tab_ref, col) * LOG2E, NEG_INF)
            cbv = jnp.where(mm < NSA_NEAR, near,
                            jnp.where(mm == NSA_NEAR, far, jnp.where(mm == NSA_NEAR + 1, NEG_INF, 0.0)))
            hi, mid, lo = _split3(cbv)
            cb_ref[0, :, cols] = hi
            cb_ref[1, :, cols] = mid
            cb_ref[2, :, cols] = lo

    qt = (head_cols(lambda h: qt_ref[0, h]).astype(F32) * (dh ** -0.5 * LOG2E)).astype(BF16)

    neg_parts = [float(v) for v in _split3_const(NEG_INF)]
    spare = lax.broadcasted_iota(jnp.int32, (16, 1), 0)
    pad_rows = (jnp.where(spare == 0, neg_parts[0], jnp.where(spare == 1, neg_parts[1],
                                                             jnp.where(spare == 2, neg_parts[2], 0.0)))
                + jnp.zeros((16, w), F32)).astype(BF16)

    def aug_rhs(pieces):
        return jnp.concatenate([qt] + list(pieces) + [pad_rows], axis=0)

    zero16 = jnp.zeros((16, w), BF16)
    rhs_plain = aug_rhs([zero16, zero16, zero16])

    def kv_start(key0):
        return pl.multiple_of(key0 + tq, LANE)

    def flash_first(m_st, acc_st, s, vt):
        m = jnp.max(s, axis=0, keepdims=True)
        m_st[...] = m
        acc_st[...] = _dot(vt, jnp.exp2(s - m).astype(BF16))

    def flash_next(m_st, acc_st, s, s_max, vt):
        m_old = m_st[...]
        m_new = jnp.maximum(m_old, s_max)
        acc_st[...] = jnp.exp2(m_old - m_new) * acc_st[...] + _dot(vt, jnp.exp2(s - m_new).astype(BF16))
        m_st[...] = m_new

    def flash_out(acc_st):
        acc = acc_st[...]
        return acc[0:dh] * (1.0 / acc[dh:dh + 1])

    n_lo = t0 // CMP_STRIDE - 9
    ni = lax.broadcasted_iota(jnp.int32, (n_cmp, NSA_CB_ROWS), 0)
    mi = lax.broadcasted_iota(jnp.int32, (n_cmp, NSA_CB_ROWS), 1)
    place = (((mi < NSA_NEAR) & (ni == mi + n_lo)) | ((mi == NSA_NEAR) & (ni < n_lo))
             | ((mi == NSA_NEAR + 1) & (ni >= n_lo + NSA_NEAR)))
    shift = jnp.where(place, 1.0, 0.0).astype(BF16)
    lhs_c = jnp.concatenate([kc_ref[0, 0], shift, shift, shift], axis=1)
    rhs_c = jnp.concatenate([qt, cb_ref[0], cb_ref[1], cb_ref[2]], axis=0)
    lcm = _dot(lhs_c, rhs_c)
    ec = jnp.exp2(lcm - jnp.max(lcm, axis=0, keepdims=True))
    sc = jnp.sum(ec, axis=0, keepdims=True)
    tl = t0 + lax.broadcasted_iota(jnp.int32, (1, w), 1) % tq
    pct = ec * jnp.where(tl >= CMP_LEN - 1, 1.0 / sc, 0.0)
    o_ct = _dot(vct_ref[0, 0], pct.astype(BF16))

    psum = pct[:, 0:tq]
    for h in range(1, nq):
        psum = psum + pct[:, h * tq:(h + 1) * tq]
    p_hi = psum.astype(BF16)
    p_lo = (psum - p_hi.astype(F32)).astype(BF16)
    jb = lax.broadcasted_iota(jnp.int32, (n_blk, n_cmp), 0)
    nb = lax.broadcasted_iota(jnp.int32, (n_blk, n_cmp), 1)
    ovl = jnp.where((CMP_STRIDE * nb < SEL_BLOCK * (jb + 1)) & (CMP_STRIDE * nb + CMP_LEN > SEL_BLOCK * jb),
                    1.0, 0.0).astype(BF16)
    imp_t = _dot(ovl, p_hi) + _dot(ovl, p_lo)
    jb2 = lax.broadcasted_iota(jnp.int32, (n_blk, tq), 0)
    cur = (t0 + lax.broadcasted_iota(jnp.int32, (n_blk, tq), 1)) // SEL_BLOCK
    forced = (jb2 == 0) | (jb2 == cur) | (jb2 == cur - 1)
    val = jnp.where(jb2 > cur, -1e9, jnp.where(forced, 1e9, imp_t))
    bits = pltpu.bitcast(val, jnp.int32)
    key = bits ^ ((bits >> 31) & 0x7FFFFFFF)
    n_causal = (t0 + tq) // SEL_BLOCK
    need = n_causal > n_sel

    def bit_body(i, tau):
        cand = tau | jnp.left_shift(1, 30 - i)
        cnt = jnp.sum(jnp.where(key >= cand, 1, 0), axis=0, keepdims=True)
        return jnp.where(cnt >= n_sel, cand, tau)

    tau = lax.fori_loop(0, jnp.where(need, 31, 0), bit_body, jnp.zeros((1, tq), jnp.int32))
    above = key > tau
    tied = key == tau
    room = n_sel - jnp.sum(jnp.where(above, 1, 0), axis=0, keepdims=True)
    tri = jnp.where(lax.broadcasted_iota(jnp.int32, (n_blk, n_blk), 1) <= lax.broadcasted_iota(jnp.int32, (n_blk, n_blk), 0),
                    1.0, 0.0).astype(BF16)
    prefix = _dot(tri, jnp.where(tied, 1.0, 0.0).astype(BF16))
    chosen = above | (tied & (prefix <= room.astype(F32))) | jnp.logical_not(need)
    neg_t = jnp.where(chosen, 0.0, NEG_INF)
    neg4 = jnp.concatenate([neg_t] * nq, axis=1)
    neg_ref[...] = neg4

    blk0 = iq * bpt
    n_far = jnp.maximum(blk0 - bpt, 0)

    def split_rows(masked, live_row):
        return [jnp.where(masked, neg_parts[i], piece.astype(F32)) for i, piece in enumerate(_split3(live_row))]

    brow = lax.broadcasted_iota(jnp.int32, (n_blk, 1), 0)
    far_parts = split_rows((neg4 < 0.0) | (brow >= n_far), far_row)
    for part in range(3):
        add_ref[part] = far_parts[part].reshape(n_blk // 16, 16, w).astype(BF16)

    near_rows = []
    for j in range(16):
        off = (j - (blk0 - bpt)) % 16
        row = neg_ref[pl.ds(jnp.clip(blk0 - bpt + off, 0, n_blk - 1), 1), :]
        near_rows.append(jnp.where(off < 2 * bpt, row, 0.0))
    near_mask = jnp.concatenate(near_rows, axis=0) < 0.0
    rhs_near = aug_rhs([p.astype(BF16) for p in split_rows(near_mask, jnp.zeros((1, w), F32))])

    near0 = kv_start(t0 - tq)
    edge0 = kv_start(jnp.maximum(t0 - 2 * tq, -tq))
    s_win = _dot(kw_ref[0, 0, pl.ds(near0, 2 * tq), :], rhs_plain) + bn_ref[...]
    s_edge = _dot(kw_ref[0, 0, pl.ds(edge0, tq), :], rhs_plain) + edge_ref[...]
    flash_first(mw_ref, accw_ref, s_win, vwt_ref[0, 0, :, pl.ds(near0, 2 * tq)])
    flash_next(mw_ref, accw_ref, s_edge, jnp.max(s_edge, axis=0, keepdims=True), vwt_ref[0, 0, :, pl.ds(edge0, tq)])
    o_wt = flash_out(accw_ref)

    s_sel = _dot(ks_ref[0, 0, pl.ds(near0, 2 * tq), :], rhs_near) + bn_ref[...]
    flash_first(m_ref, acc_ref, s_sel, vst_ref[0, 0, :, pl.ds(near0, 2 * tq)])

    bpf = tkf // SEL_BLOCK
    n_tiles = n_far // bpf

    def qk_stage(c, s_buf, m_buf):
        chunk = (c * bpf) // 16
        s = _dot(ks_ref[0, 0, pl.ds(kv_start(c * tkf), tkf), :],
                 aug_rhs([add_ref[0, chunk], add_ref[1, chunk], add_ref[2, chunk]]))
        s_buf[...] = s
        m_buf[...] = jnp.max(s, axis=0, keepdims=True)

    def sm_stage(c, s_buf, m_buf):
        flash_next(m_ref, acc_ref, s_buf[...], m_buf[...], vst_ref[0, 0, :, pl.ds(kv_start(c * tkf), tkf)])

    @pl.when(n_tiles > 0)
    def _():
        qk_stage(0, s0_ref, m0_ref)

    def far_body(i, carry):
        qk_stage(2 * i + 1, s1_ref, m1_ref)
        sm_stage(2 * i, s0_ref, m0_ref)
        qk_stage(2 * i + 2, s0_ref, m0_ref)
        sm_stage(2 * i + 1, s1_ref, m1_ref)
        return carry

    lax.fori_loop(0, (n_tiles + 1) // 2, far_body, 0)
    o_st = flash_out(acc_ref)

    sg = jax.nn.sigmoid(gate_ref[0, 0])
    outs = []
    for h in range(nq):
        cols = slice(h * tq, (h + 1) * tq)
        outs.append(sg[3 * h:3 * h + 1] * o_ct[:, cols] + sg[3 * h + 1:3 * h + 2] * o_st[:, cols]
                    + sg[3 * h + 2:3 * h + 3] * o_wt[:, cols])
    o_ref[0] = jnp.concatenate(outs, axis=0).T.astype(o_ref.dtype)


def _nsa_attention(tab, qt, kc, vct, ks, vst, kw, vwt, gate):
    b, _, dh, t = qt.shape
    g = NSA_KV_HEADS
    nq = NSA_QPG
    tq = NSA_TQ
    w = nq * tq
    assert t % tq == 0 and tq >= REL_MAX_DIST and NSA_WINDOW == 2 * tq and tq % SEL_BLOCK == 0
    assert NSA_TKF % SEL_BLOCK == 0 and t >= NSA_TKF and tq % LANE == 0
    assert NSA_NEAR + 1 < NSA_CB_ROWS
    n_cmp = kc.shape[2]
    n_blk = t // SEL_BLOCK
    n_sel = min(SEL_COUNT, n_blk)
    assert n_sel >= 3
    tp = ks.shape[2]
    bpf = NSA_TKF // SEL_BLOCK
    assert tp == t + tq and vst.shape[3] == tp and 16 % bpf == 0 and n_blk % 16 == 0 and tq % NSA_TKF == 0
    assert ks.shape[3] == 2 * dh and vst.shape[2] == dh + 16
    kspec = pl.BlockSpec((1, 1, tp, 2 * dh), lambda j, i, k: (i, j, 0, 0))
    vtspec = pl.BlockSpec((1, 1, dh + 16, tp), lambda j, i, k: (i, j, 0, 0))
    return pl.pallas_call(
        functools.partial(_nsa_attn_kernel, n_sel=n_sel),
        grid=(g, b, t // tq),
        in_specs=[pl.BlockSpec(memory_space=pltpu.SMEM),
                  pl.BlockSpec((1, nq, dh, tq), lambda j, i, k: (i, j, 0, k)),
                  pl.BlockSpec((1, 1, n_cmp, dh), lambda j, i, k: (i, j, 0, 0)),
                  pl.BlockSpec((1, 1, dh, n_cmp), lambda j, i, k: (i, j, 0, 0)),
                  kspec, vtspec, kspec, vtspec,
                  pl.BlockSpec((1, 1, 16, tq), lambda j, i, k: (i, j, 0, k))],
        out_specs=pl.BlockSpec((1, tq, nq * dh), lambda j, i, k: (i, k, j)),
        out_shape=jax.ShapeDtypeStruct((b, t, g * nq * dh), BF16),
        scratch_shapes=[pltpu.VMEM((2 * tq, w), F32),
                        pltpu.VMEM((tq, w), F32),
                        pltpu.VMEM((3, NSA_CB_ROWS, w), BF16),
                        pltpu.VMEM((n_blk, w), F32),
                        pltpu.VMEM((3, n_blk // 16, 16, w), BF16),
                        pltpu.VMEM((1, w), F32),
                        pltpu.VMEM((dh + 16, w), F32),
                        pltpu.VMEM((1, w), F32),
                        pltpu.VMEM((dh + 16, w), F32),
                        pltpu.VMEM((NSA_TKF, w), F32), pltpu.VMEM((NSA_TKF, w), F32),
                        pltpu.VMEM((1, w), F32), pltpu.VMEM((1, w), F32)],
        compiler_params=_cparams(("arbitrary", "arbitrary", "arbitrary")),
        name="nsa_attn",
    )(tab, qt, kc, vct, ks, vst, kw, vwt, gate)


def _nsa_layer(x, w_in, w_out, cmp_pos, cmp_w1, cmp_w2, rel_bias, gain, bias, alpha):
    b, t, d = x.shape
    g, nq, dh = NSA_KV_HEADS, NSA_QPG, HEAD_DIM
    kvw = g * dh
    x2 = x.reshape(b * t, d)
    off = ATTN_HEADS * dh
    cols_t = np.concatenate([np.arange(off), np.arange(off + 3 * kvw, off + 4 * kvw),
                             np.arange(off + 5 * kvw, off + 6 * kvw)])
    cols_k = np.concatenate([np.arange(off + 2 * kvw, off + 3 * kvw), np.arange(off + 4 * kvw, off + 5 * kvw)])
    cols_f = np.concatenate([np.arange(off, off + 2 * kvw), np.arange(off + 6 * kvw, off + 6 * kvw + 3 * ATTN_HEADS)])
    n_pad = -len(cols_f) % LANE
    w_f = jnp.pad(w_in[:, cols_f], ((0, 0), (0, n_pad))).astype(BF16)
    ht = _proj_t(x, w_in[:, cols_t].T.astype(BF16), tn=512)
    hk = _proj(x2, w_in[:, cols_k].astype(BF16), tn=len(cols_k), out_dtype=BF16).reshape(b, t, len(cols_k))
    hf = _proj(x2, w_f, tn=len(cols_f) + n_pad).reshape(b, t, len(cols_f) + n_pad)

    def heads(src, lo, n_heads):
        return src[:, :, lo:lo + n_heads * dh].reshape(b, t, n_heads, dh).transpose(0, 2, 1, 3)

    def heads_t(lo, n_heads):
        return ht[:, lo:lo + n_heads * dh, :].reshape(b, n_heads, dh, t)

    qt = heads_t(0, ATTN_HEADS)
    zk = heads(hf, 0, g).reshape(b, g, t // CMP_STRIDE, CMP_STRIDE * dh)
    zv = heads(hf, kvw, g).reshape(b, g, t // CMP_STRIDE, CMP_STRIDE * dh)
    tp = t + NSA_TQ
    is_pad = np.arange(tp) < NSA_TQ
    slot = (np.arange(tp) - NSA_TQ) // SEL_BLOCK % 16
    onehot = ((np.arange(16)[None, :] == slot[:, None]) & ~is_pad[:, None]).astype(np.float32)
    pad_cols = ((np.arange(16)[None, :] < 3) & is_pad[:, None]).astype(np.float32)
    k_cols = jnp.asarray(np.concatenate([onehot] * 3 + [pad_cols], axis=1), BF16)
    v_rows = jnp.asarray((np.arange(16)[:, None] == 0) * np.ones((1, tp)), BF16)

    def aug_k(z):
        z = jnp.pad(z, ((0, 0), (0, 0), (NSA_TQ, 0), (0, 0)))
        return jnp.concatenate([z, jnp.broadcast_to(k_cols, (b, g, tp, dh))], axis=3)

    def aug_vt(z):
        z = jnp.pad(z, ((0, 0), (0, 0), (0, 0), (NSA_TQ, 0)))
        return jnp.concatenate([z, jnp.broadcast_to(v_rows, (b, g, 16, tp))], axis=2)

    ks = aug_k(heads(hk, 0, g))
    vst = aug_vt(heads_t(off, g))
    kw = aug_k(heads(hk, kvw, g))
    vwt = aug_vt(heads_t(off + kvw, g))
    gate = hf[:, :, 2 * kvw:2 * kvw + 3 * ATTN_HEADS].reshape(b, t, g, 3 * nq)
    gate = jnp.pad(gate.transpose(0, 2, 3, 1), ((0, 0), (0, 0), (0, 16 - 3 * nq), (0, 0)))

    pos = cmp_pos.reshape(2, 2, 1, CMP_STRIDE * dh)
    w1 = cmp_w1.reshape(2, 2, CMP_STRIDE * dh, cmp_w1.shape[-1]).astype(BF16)
    kc, vct = _nsa_compress(zk, zv, pos, w1, cmp_w2.astype(BF16))
    o = _nsa_attention(rel_bias, qt, kc, vct, ks, vst, kw, vwt, gate)
    y = _proj_ln(o.reshape(b * t, d), w_out.astype(BF16), x2, gain, bias, alpha)
    return y.reshape(b, t, d)


def _hgrn_kernel(lbp_ref, gain_ref, zq_ref, zf_ref, zi_ref, zg_ref, o_ref, st_ref, *, layer):
    C = HGRN_TC
    t = zq_ref.shape[1]
    kdim = zq_ref.shape[2]

    p = lbp_ref[...]
    e = jnp.exp(p - jnp.max(p, axis=0, keepdims=True))
    sm = e / jnp.sum(e, axis=0, keepdims=True)
    cs = sm[0:1]
    for r in range(1, layer + 1):
        cs = cs + sm[r:r + 1]
    lb = cs - sm[0:1]
    log_lb = jnp.log(lb)
    log_1m = jnp.log1p(-lb)
    gain = gain_ref[...]

    st_ref[...] = jnp.zeros(st_ref.shape, F32)

    rr = lax.broadcasted_iota(jnp.int32, (C, C), 0)
    cc = lax.broadcasted_iota(jnp.int32, (C, C), 1)
    srow = lax.broadcasted_iota(jnp.int32, (C, 1), 0)
    halves = [1 << i for i in range(C.bit_length() - 1)]
    small = [h for h in halves if h < 8]
    sel_rows = [jnp.where(cc <= rr, 1.0, 0.0).astype(BF16)]
    sel_rows += [jnp.where(cc <= (rr // (2 * h)) * (2 * h) + h - 1, 1.0, 0.0).astype(BF16) for h in small]
    cum_sel = jnp.concatenate(sel_rows, axis=0)
    right = [(srow // h) % 2 == 1 for h in halves]
    same_blk = [rr // (2 * h) == cc // (2 * h) for h in halves]
    eye = rr == cc

    def chunk(c, carry):
        r0 = pl.multiple_of(c * C, C)
        zq = zq_ref[0, pl.ds(r0, C), :]
        zf = zf_ref[0, pl.ds(r0, C), :]
        v = zi_ref[0, pl.ds(r0, C), :]
        zg = zg_ref[0, pl.ds(r0, C), :]
        q = jax.nn.silu(zq)
        log_f = jnp.logaddexp(log_lb, log_1m + jax.nn.log_sigmoid(zf))
        kk = (1.0 - lb) * jax.nn.sigmoid(-zf)
        vb = v.astype(BF16)

        g_hi, g_mid, g_lo = _split3(log_f)
        cums = _dot(cum_sel, g_hi) + _dot(cum_sel, g_mid) + _dot(cum_sel, g_lo)
        bcum = cums[0:C]
        b_last = bcum[C - 1:C, :]

        st = st_ref[...]
        o = _dot_nt((q * jnp.exp(bcum)).astype(BF16), st.astype(BF16))

        a = jnp.where(eye, _dot_nt(q.astype(BF16), kk.astype(BF16)), 0.0)
        for lvl, h in enumerate(halves):
            if h in small:
                b_ref = cums[(lvl + 1) * C:(lvl + 2) * C]
            else:
                b_ref = jnp.broadcast_to(bcum.reshape(C // (2 * h), 2 * h, kdim)[:, h - 1:h, :],
                                         (C // (2 * h), 2 * h, kdim)).reshape(C, kdim)
            e = jnp.exp(-jnp.abs(bcum - b_ref))
            q_side = jnp.where(right[lvl], q * e, 0.0)
            k_side = jnp.where(right[lvl], 0.0, kk * e)
            a = a + jnp.where(same_blk[lvl], _dot_nt(q_side.astype(BF16), k_side.astype(BF16)), 0.0)
        o = o + _dot(a.astype(BF16), vb)

        kd = kk * jnp.exp(b_last - bcum)
        st_ref[...] = st * jnp.exp(b_last) + _dot(v.T.astype(BF16), kd.astype(BF16))

        o = o * lax.rsqrt(jnp.mean(o * o, axis=-1, keepdims=True) + RMS_EPS) * gain
        o = o * jax.nn.silu(zg)
        o_ref[0, pl.ds(r0, C), :] = o.astype(o_ref.dtype)
        return carry

    lax.fori_loop(0, t // C, chunk, 0, unroll=4)


def _hgrn_layer(x, w_in, w_out, norm_gain, lb_param, layer, gain, bias, alpha):
    b, t, d = x.shape
    nh = HGRN_HEADS
    kd = d // nh
    x2 = x.reshape(b * t, d)
    h = _proj(x2, w_in.astype(BF16), tn=1024).reshape(b, t, 4 * d)

    def zspec(part):
        return pl.BlockSpec((1, t, kd), lambda i, j: (i, 0, part * nh + j))

    o = pl.pallas_call(
        functools.partial(_hgrn_kernel, layer=layer),
        grid=(b, nh),
        in_specs=[pl.BlockSpec((lb_param.shape[0], kd), lambda i, j: (0, j)),
                  pl.BlockSpec((1, kd), lambda i, j: (0, 0)),
                  zspec(0), zspec(1), zspec(2), zspec(3)],
        out_specs=pl.BlockSpec((1, t, kd), lambda i, j: (i, 0, j)),
        out_shape=jax.ShapeDtypeStruct((b, t, d), BF16),
        scratch_shapes=[pltpu.VMEM((kd, kd), F32)],
        compiler_params=_cparams(("parallel", "parallel")),
        name="hgrn",
    )(lb_param, norm_gain.reshape(1, kd), h, h, h, h)
    y = _proj_ln(o.reshape(b * t, d), w_out.astype(BF16), x2, gain, bias, alpha)
    return y.reshape(b, t, d)


def _swa_kernel(tab_ref, sink_ref, q_ref, k_ref, v_ref, o_ref, bd_ref, bp_ref):
    L = SWA_WINDOW
    nq = SWA_QPG
    dh = HEAD_DIM
    g = pl.program_id(1)
    n = pl.program_id(2)
    ii = lax.broadcasted_iota(jnp.int32, (L, L), 0)
    jj = lax.broadcasted_iota(jnp.int32, (L, L), 1)

    @pl.when(n == 0)
    def _init():
        b_diag = _t5_bucket(ii - jj)
        b_prev = _t5_bucket(ii - jj + L)
        for h in range(nq):
            bd_ref[h] = _bias_lookup(b_diag, tab_ref, g * nq + h)
            bp_ref[h] = _bias_lookup(b_prev, tab_ref, g * nq + h)

    qs = (q_ref[0].astype(F32) * (dh ** -0.5)).astype(BF16).reshape(nq * L, dh)
    cur = pl.multiple_of(n * L, L)
    prev = pl.multiple_of(jnp.maximum(n - 1, 0) * L, L)
    k_c = k_ref[0, 0, pl.ds(cur, L), :]
    v_c = v_ref[0, 0, pl.ds(cur, L), :]
    k_p = k_ref[0, 0, pl.ds(prev, L), :]
    v_p = v_ref[0, 0, pl.ds(prev, L), :]

    s_c = _dot_nt(qs, k_c).reshape(nq, L, L) + bd_ref[...]
    s_p = _dot_nt(qs, k_p).reshape(nq, L, L) + bp_ref[...]
    s_c = jnp.where((ii >= jj)[None], s_c, NEG_INF)
    s_p = jnp.where(((jj > ii) & (n > 0))[None], s_p, NEG_INF)
    sink = _per_head_scalar(nq, lambda h: sink_ref[g * nq + h])
    m = jnp.maximum(jnp.maximum(jnp.max(s_c, axis=-1, keepdims=True), jnp.max(s_p, axis=-1, keepdims=True)), sink)
    e_c = jnp.exp(s_c - m)
    e_p = jnp.exp(s_p - m)
    den = jnp.sum(e_c, axis=-1, keepdims=True) + jnp.sum(e_p, axis=-1, keepdims=True) + jnp.exp(sink - m)
    pv = (_dot(e_c.reshape(nq * L, L).astype(BF16), v_c) + _dot(e_p.reshape(nq * L, L).astype(BF16), v_p))
    o = pv.reshape(nq, L, dh) / den
    o_ref[0] = jnp.concatenate([o[h] for h in range(nq)], axis=1).astype(o_ref.dtype)


def _swa_layer(x, w_in, w_out, sinks, rel_bias, gain, bias, alpha):
    b, t, d = x.shape
    kvh, nq, dh, L = SWA_KV_HEADS, SWA_QPG, HEAD_DIM, SWA_WINDOW
    assert t % L == 0 and L >= REL_MAX_DIST
    x2 = x.reshape(b * t, d)
    n_in = w_in.shape[1]
    h = _proj(x2, w_in.astype(BF16), tn=n_in // 2, out_dtype=BF16).reshape(b, t, n_in)

    def heads(lo, n_heads):
        return h[:, :, lo:lo + n_heads * dh].reshape(b, t, n_heads, dh).transpose(0, 2, 1, 3)

    q = heads(0, ATTN_HEADS)
    k = heads(ATTN_HEADS * dh, kvh)
    v = heads(ATTN_HEADS * dh + kvh * dh, kvh)
    kvspec = pl.BlockSpec((1, 1, t, dh), lambda i, j, n: (i, j, 0, 0))
    o = pl.pallas_call(
        _swa_kernel,
        grid=(b, kvh, t // L),
        in_specs=[pl.BlockSpec(memory_space=pltpu.SMEM),
                  pl.BlockSpec(memory_space=pltpu.SMEM),
                  pl.BlockSpec((1, nq, L, dh), lambda i, j, n: (i, j, n, 0)),
                  kvspec, kvspec],
        out_specs=pl.BlockSpec((1, L, nq * dh), lambda i, j, n: (i, n, j)),
        out_shape=jax.ShapeDtypeStruct((b, t, d), BF16),
        scratch_shapes=[pltpu.VMEM((nq, L, L), F32), pltpu.VMEM((nq, L, L), F32)],
        compiler_params=_cparams(("parallel", "parallel", "arbitrary")),
        name="swa_attn",
    )(rel_bias, sinks, q, k, v)
    y = _proj_ln(o.reshape(b * t, d), w_out.astype(BF16), x2, gain, bias, alpha)
    return y.reshape(b, t, d)


def kernel(x, rel_bias, ln_gain, ln_bias, ffn1_w_gate, ffn1_w_up, ffn1_w_down, ffn2_w_gate, ffn2_w_up,
           ffn2_w_down, nsa_w_in, nsa_w_out, nsa_cmp_pos, nsa_cmp_w1, nsa_cmp_w2, hgrn_w_in, hgrn_w_out,
           hgrn_norm_gain, hgrn_lb, swa_w_in, swa_w_out, swa_sinks):
    depth = ln_gain.shape[0]
    alpha = (2.0 * depth) ** 0.25
    b, t, d = x.shape

    def ffn(x, wg, wu, wd, gain, bias):
        y = _ffn_ln(x.reshape(b * t, d), wg.astype(BF16), wu.astype(BF16), wd.astype(BF16), gain, bias, alpha)
        return y.reshape(b, t, d)

    for i in range(depth):
        x = ffn(x, ffn1_w_gate[i], ffn1_w_up[i], ffn1_w_down[i], ln_gain[i, 0], ln_bias[i, 0])
        kind, slot = i % N_MIXERS, i // N_MIXERS
        if kind == 0:
            x = _nsa_layer(x, nsa_w_in[slot], nsa_w_out[slot], nsa_cmp_pos[slot], nsa_cmp_w1[slot],
                           nsa_cmp_w2[slot], rel_bias, ln_gain[i, 1], ln_bias[i, 1], alpha)
        elif kind == 1:
            x = _hgrn_layer(x, hgrn_w_in[slot], hgrn_w_out[slot], hgrn_norm_gain[slot], hgrn_lb, i,
                            ln_gain[i, 1], ln_bias[i, 1], alpha)
        else:
            x = _swa_layer(x, swa_w_in[slot], swa_w_out[slot], swa_sinks[slot], rel_bias,
                           ln_gain[i, 1], ln_bias[i, 1], alpha)
        x = ffn(x, ffn2_w_gate[i], ffn2_w_up[i], ffn2_w_down[i], ln_gain[i, 2], ln_bias[i, 2])
    return x
```

```python
import functools
import math

import jax
import jax.numpy as jnp
import numpy as np
from jax import lax
from jax.experimental import pallas as pl
from jax.experimental.pallas import tpu as pltpu

F32 = jnp.float32
BF16 = jnp.bfloat16

DEPTH = 4
N_MIXERS = 3
REL_BUCKETS = 32
REL_EXACT = REL_BUCKETS // 2
REL_MAX_DIST = 128
ATTN_HEADS = 16
HEAD_DIM = 64
NSA_KV_HEADS = 4
NSA_QPG = ATTN_HEADS // NSA_KV_HEADS
CMP_STRIDE = 16
CMP_LEN = 2 * CMP_STRIDE
SEL_BLOCK = 64
SEL_COUNT = 16
NSA_WINDOW = 512
HGRN_HEADS = 8
HGRN_CHUNK = 64
HGRN_TC = 128
SWA_KV_HEADS = 2
SWA_QPG = ATTN_HEADS // SWA_KV_HEADS
SWA_WINDOW = 128
LN_EPS = 1e-5
RMS_EPS = 1e-6
NEG_INF = -1e30
LOG2E = 1.4426950408889634

LANE = 128
VMEM_LIMIT = 48 * 1024 * 1024
NSA_TQ = 256
NSA_TKF = 256
NSA_NEAR = NSA_TQ // CMP_STRIDE + 8
NSA_CB_ROWS = 32


def _cparams(sem):
    return pltpu.CompilerParams(dimension_semantics=sem, vmem_limit_bytes=VMEM_LIMIT)


def _dot(a, b):
    return jnp.dot(a, b, preferred_element_type=F32)


def _dot_nt(a, b):
    return lax.dot_general(a, b, (((1,), (1,)), ((), ())), preferred_element_type=F32)


def _layer_norm_rows(y, g, b):
    mu = jnp.mean(y, axis=-1, keepdims=True)
    yc = y - mu
    var = jnp.mean(yc * yc, axis=-1, keepdims=True)
    return yc * lax.rsqrt(var + LN_EPS) * g + b


def _ffn_ln_kernel(x_ref, wg_ref, wu_ref, wd_ref, g_ref, b_ref, o_ref, xb_ref, acc_ref, *, alpha):
    j = pl.program_id(1)

    @pl.when(j == 0)
    def _():
        xb_ref[...] = x_ref[...].astype(BF16)
        acc_ref[...] = jnp.zeros_like(acc_ref)

    xb = xb_ref[...]
    gate = _dot(xb, wg_ref[...])
    up = _dot(xb, wu_ref[...])
    h = (jax.nn.silu(gate) * up).astype(BF16)
    acc_ref[...] += _dot(h, wd_ref[...])

    @pl.when(j == pl.num_programs(1) - 1)
    def _():
        y = alpha * x_ref[...] + 0.5 * acc_ref[...]
        o_ref[...] = _layer_norm_rows(y, g_ref[...], b_ref[...])


def _ffn_ln(x, wg, wu, wd, gain, bias, alpha, tm=512, tf=1408):
    n, d = x.shape
    f = wg.shape[1]
    tm = min(tm, n)
    assert n % tm == 0 and f % tf == 0
    return pl.pallas_call(
        functools.partial(_ffn_ln_kernel, alpha=alpha),
        grid=(n // tm, f // tf),
        in_specs=[
            pl.BlockSpec((tm, d), lambda i, j: (i, 0)),
            pl.BlockSpec((d, tf), lambda i, j: (0, j)),
            pl.BlockSpec((d, tf), lambda i, j: (0, j)),
            pl.BlockSpec((tf, d), lambda i, j: (j, 0)),
            pl.BlockSpec((1, d), lambda i, j: (0, 0)),
            pl.BlockSpec((1, d), lambda i, j: (0, 0)),
        ],
        out_specs=pl.BlockSpec((tm, d), lambda i, j: (i, 0)),
        out_shape=jax.ShapeDtypeStruct((n, d), F32),
        scratch_shapes=[pltpu.VMEM((tm, d), BF16), pltpu.VMEM((tm, d), F32)],
        compiler_params=_cparams(("parallel", "arbitrary")),
        name="ffn_ln",
    )(x, wg, wu, wd, gain.reshape(1, d), bias.reshape(1, d))


def _proj_kernel(x_ref, w_ref, o_ref, xb_ref):
    @pl.when(pl.program_id(1) == 0)
    def _():
        xb_ref[...] = x_ref[...].astype(BF16)

    o_ref[...] = _dot(xb_ref[...], w_ref[...]).astype(o_ref.dtype)


def _proj(x, w, tn, tm=1024, out_dtype=F32):
    n, k = x.shape
    m = w.shape[1]
    tm = min(tm, n)
    assert n % tm == 0 and m % tn == 0
    return pl.pallas_call(
        _proj_kernel,
        grid=(n // tm, m // tn),
        in_specs=[pl.BlockSpec((tm, k), lambda i, j: (i, 0)),
                  pl.BlockSpec((k, tn), lambda i, j: (0, j))],
        out_specs=pl.BlockSpec((tm, tn), lambda i, j: (i, j)),
        out_shape=jax.ShapeDtypeStruct((n, m), out_dtype),
        scratch_shapes=[pltpu.VMEM((tm, k), BF16)],
        compiler_params=_cparams(("parallel", "arbitrary")),
        name="proj",
    )(x, w)


def _proj_t_kernel(x_ref, wt_ref, o_ref, xb_ref):
    @pl.when(pl.program_id(2) == 0)
    def _():
        xb_ref[...] = x_ref[0].astype(BF16)

    o_ref[0] = _dot_nt(wt_ref[...], xb_ref[...]).astype(o_ref.dtype)


def _proj_t(x, wt, tn, tm=1024, out_dtype=BF16):
    b, t, k = x.shape
    m = wt.shape[0]
    tm = min(tm, t)
    assert t % tm == 0 and m % tn == 0
    return pl.pallas_call(
        _proj_t_kernel,
        grid=(b, t // tm, m // tn),
        in_specs=[pl.BlockSpec((1, tm, k), lambda i, j, n: (i, j, 0)),
                  pl.BlockSpec((tn, k), lambda i, j, n: (n, 0))],
        out_specs=pl.BlockSpec((1, tn, tm), lambda i, j, n: (i, n, j)),
        out_shape=jax.ShapeDtypeStruct((b, m, t), out_dtype),
        scratch_shapes=[pltpu.VMEM((tm, k), BF16)],
        compiler_params=_cparams(("parallel", "parallel", "arbitrary")),
        name="proj_t",
    )(x, wt)


def _proj_ln_kernel(a_ref, w_ref, r_ref, g_ref, b_ref, o_ref, *, alpha):
    y = _dot(a_ref[...].astype(BF16), w_ref[...])
    o_ref[...] = _layer_norm_rows(alpha * r_ref[...] + y, g_ref[...], b_ref[...])


def _proj_ln(a, w, res, gain, bias, alpha, tm=512):
    n, k = a.shape
    d = w.shape[1]
    tm = min(tm, n)
    assert n % tm == 0
    return pl.pallas_call(
        functools.partial(_proj_ln_kernel, alpha=alpha),
        grid=(n // tm,),
        in_specs=[pl.BlockSpec((tm, k), lambda i: (i, 0)),
                  pl.BlockSpec((k, d), lambda i: (0, 0)),
                  pl.BlockSpec((tm, d), lambda i: (i, 0)),
                  pl.BlockSpec((1, d), lambda i: (0, 0)),
                  pl.BlockSpec((1, d), lambda i: (0, 0))],
        out_specs=pl.BlockSpec((tm, d), lambda i: (i, 0)),
        out_shape=jax.ShapeDtypeStruct((n, d), F32),
        compiler_params=_cparams(("parallel",)),
        name="proj_ln",
    )(a, w, res, gain.reshape(1, d), bias.reshape(1, d))


def _t5_bucket(dist):
    n = jnp.maximum(dist, 0)
    nf = jnp.maximum(n, 1).astype(F32)
    large = REL_EXACT + (jnp.log(nf / REL_EXACT) / math.log(REL_MAX_DIST / REL_EXACT)
                         * (REL_BUCKETS - REL_EXACT)).astype(jnp.int32)
    return jnp.where(n < REL_EXACT, n, jnp.minimum(large, REL_BUCKETS - 1))


def _bias_lookup(bucket, tab_ref, col):
    out = jnp.zeros(bucket.shape, F32)
    for k in range(REL_BUCKETS):
        out = jnp.where(bucket == k, tab_ref[k, col], out)
    return out


def _split3_const(value):
    x = np.float32(value)
    hi = np.float32(np.asarray(x, dtype=jnp.bfloat16))
    mid = np.float32(np.asarray(x - hi, dtype=jnp.bfloat16))
    lo = np.float32(np.asarray(x - hi - mid, dtype=jnp.bfloat16))
    return hi, mid, lo


def _split3(x):
    hi = x.astype(BF16)
    r1 = x - hi.astype(F32)
    mid = r1.astype(BF16)
    lo = (r1 - mid.astype(F32)).astype(BF16)
    return hi, mid, lo


def _nsa_compress_kernel(zk_ref, zv_ref, pos_ref, w1_ref, w2_ref, w2t_ref, kc_ref, vct_ref):
    def hidden(idx, z_ref):
        z = z_ref[0, 0]
        u = _dot((z + pos_ref[idx, 0]).astype(BF16), w1_ref[idx, 0])
        v = _dot((z + pos_ref[idx, 1]).astype(BF16), w1_ref[idx, 1])
        return jax.nn.gelu(u + pltpu.roll(v, z.shape[0] - 1, 0)).astype(BF16)

    kc_ref[0, 0] = _dot(hidden(0, zk_ref), w2_ref[0]).astype(kc_ref.dtype)
    vct_ref[0, 0] = _dot_nt(w2t_ref[1], hidden(1, zv_ref)).astype(vct_ref.dtype)


def _nsa_compress(zk, zv, pos, w1, w2):
    b, g, nc, kd = zk.shape
    hid = w1.shape[-1]
    dh = w2.shape[-1]
    zspec = pl.BlockSpec((1, 1, nc, kd), lambda i, j: (i, j, 0, 0))
    return pl.pallas_call(
        _nsa_compress_kernel,
        grid=(b, g),
        in_specs=[zspec, zspec,
                  pl.BlockSpec((2, 2, 1, kd), lambda i, j: (0, 0, 0, 0)),
                  pl.BlockSpec((2, 2, kd, hid), lambda i, j: (0, 0, 0, 0)),
                  pl.BlockSpec((2, hid, dh), lambda i, j: (0, 0, 0)),
                  pl.BlockSpec((2, dh, hid), lambda i, j: (0, 0, 0))],
        out_specs=[pl.BlockSpec((1, 1, nc, dh), lambda i, j: (i, j, 0, 0)),
                   pl.BlockSpec((1, 1, dh, nc), lambda i, j: (i, j, 0, 0))],
        out_shape=[jax.ShapeDtypeStruct((b, g, nc, dh), BF16), jax.ShapeDtypeStruct((b, g, dh, nc), BF16)],
        compiler_params=_cparams(("parallel", "parallel")),
        name="nsa_compress",
    )(zk, zv, pos, w1, w2, w2.transpose(0, 2, 1))


def _nsa_attn_kernel(tab_ref, qt_ref, kc_ref, vct_ref, ks_ref, vst_ref, kw_ref, vwt_ref, gate_ref, o_ref,
                     bn_ref, edge_ref, cb_ref, neg_ref, add_ref, m_ref, acc_ref, mw_ref, accw_ref,
                     s0_ref, s1_ref, m0_ref, m1_ref, *, n_sel):
    tq = NSA_TQ
    tkf = NSA_TKF
    nq = NSA_QPG
    dh = HEAD_DIM
    w = nq * tq
    g = pl.program_id(0)
    iq = pl.program_id(2)
    t0 = iq * tq
    n_cmp = kc_ref.shape[2]
    n_blk = neg_ref.shape[0]
    bpt = tq // SEL_BLOCK
    far_bucket = REL_BUCKETS - 1

    def head_cols(fn):
        return jnp.concatenate([fn(h) for h in range(nq)], axis=1)

    far_row = head_cols(lambda h: jnp.full((1, tq), tab_ref[far_bucket, g * nq + h] * LOG2E, F32))

    @pl.when((iq == 0) & (pl.program_id(1) == 0))
    def _init():
        jj = lax.broadcasted_iota(jnp.int32, (tq, tq), 0)
        ii = lax.broadcasted_iota(jnp.int32, (tq, tq), 1)
        d = ii - jj
        b_diag = _t5_bucket(d)
        b_prev = _t5_bucket(d + tq)
        mm = lax.broadcasted_iota(jnp.int32, (NSA_CB_ROWS, tq), 0)
        i2 = lax.broadcasted_iota(jnp.int32, (NSA_CB_ROWS, tq), 1)
        dn = i2 - CMP_STRIDE * mm + (9 * CMP_STRIDE - CMP_LEN + 1)
        b_near = _t5_bucket(dn)
        for h in range(nq):
            col = g * nq + h
            far = tab_ref[far_bucket, col] * LOG2E
            cols = slice(h * tq, (h + 1) * tq)
            bn_ref[0:tq, cols] = _bias_lookup(b_prev, tab_ref, col) * LOG2E
            bn_ref[tq:2 * tq, cols] = jnp.where(d >= 0, _bias_lookup(b_diag, tab_ref, col) * LOG2E, NEG_INF)
            edge_ref[:, cols] = jnp.where(jj > ii, far, NEG_INF)
            near = jnp.where(dn >= 0, _bias_lookup(b_near, tab_ref, col) * LOG2E, NEG_INF)
            cbv = jnp.where(mm < NSA_NEAR, near,
                            jnp.where(mm == NSA_NEAR, far, jnp.where(mm == NSA_NEAR + 1, NEG_INF, 0.0)))
            hi, mid, lo = _split3(cbv)
            cb_ref[0, :, cols] = hi
            cb_ref[1, :, cols] = mid
            cb_ref[2, :, cols] = lo

    qt = (head_cols(lambda h: qt_ref[0, h]).astype(F32) * (dh ** -0.5 * LOG2E)).astype(BF16)

    neg_parts = [float(v) for v in _split3_const(NEG_INF)]
    spare = lax.broadcasted_iota(jnp.int32, (16, 1), 0)
    pad_rows = (jnp.where(spare == 0, neg_parts[0], jnp.where(spare == 1, neg_parts[1],
                                                             jnp.where(spare == 2, neg_parts[2], 0.0)))
                + jnp.zeros((16, w), F32)).astype(BF16)

    def aug_rhs(pieces):
        return jnp.concatenate([qt] + list(pieces) + [pad_rows], axis=0)

    zero16 = jnp.zeros((16, w), BF16)
    rhs_plain = aug_rhs([zero16, zero16, zero16])

    def kv_start(key0):
        return pl.multiple_of(key0 + tq, LANE)

    def flash_first(m_st, acc_st, s, vt):
        m = jnp.max(s, axis=0, keepdims=True)
        m_st[...] = m
        acc_st[...] = _dot(vt, jnp.exp2(s - m).astype(BF16))

    def flash_next(m_st, acc_st, s, s_max, vt):
        m_old = m_st[...]
        m_new = jnp.maximum(m_old, s_max)
        acc_st[...] = jnp.exp2(m_old - m_new) * acc_st[...] + _dot(vt, jnp.exp2(s - m_new).astype(BF16))
        m_st[...] = m_new

    def flash_out(acc_st):
        acc = acc_st[...]
        return acc[0:dh] * (1.0 / acc[dh:dh + 1])

    n_lo = t0 // CMP_STRIDE - 9
    ni = lax.broadcasted_iota(jnp.int32, (n_cmp, NSA_CB_ROWS), 0)
    mi = lax.broadcasted_iota(jnp.int32, (n_cmp, NSA_CB_ROWS), 1)
    place = (((mi < NSA_NEAR) & (ni == mi + n_lo)) | ((mi == NSA_NEAR) & (ni < n_lo))
             | ((mi == NSA_NEAR + 1) & (ni >= n_lo + NSA_NEAR)))
    shift = jnp.where(place, 1.0, 0.0).astype(BF16)
    lhs_c = jnp.concatenate([kc_ref[0, 0], shift, shift, shift], axis=1)
    rhs_c = jnp.concatenate([qt, cb_ref[0], cb_ref[1], cb_ref[2]], axis=0)
    lcm = _dot(lhs_c, rhs_c)
    ec = jnp.exp2(lcm - jnp.max(lcm, axis=0, keepdims=True))
    sc = jnp.sum(ec, axis=0, keepdims=True)
    tl = t0 + lax.broadcasted_iota(jnp.int32, (1, w), 1) % tq
    pct = ec * jnp.where(tl >= CMP_LEN - 1, 1.0 / sc, 0.0)
    o_ct = _dot(vct_ref[0, 0], pct.astype(BF16))

    psum = pct[:, 0:tq]
    for h in range(1, nq):
        psum = psum + pct[:, h * tq:(h + 1) * tq]
    p_hi = psum.astype(BF16)
    p_lo = (psum - p_hi.astype(F32)).astype(BF16)
    jb = lax.broadcasted_iota(jnp.int32, (n_blk, n_cmp), 0)
    nb = lax.broadcasted_iota(jnp.int32, (n_blk, n_cmp), 1)
    ovl = jnp.where((CMP_STRIDE * nb < SEL_BLOCK * (jb + 1)) & (CMP_STRIDE * nb + CMP_LEN > SEL_BLOCK * jb),
                    1.0, 0.0).astype(BF16)
    imp_t = _dot(ovl, p_hi) + _dot(ovl, p_lo)
    jb2 = lax.broadcasted_iota(jnp.int32, (n_blk, tq), 0)
    cur = (t0 + lax.broadcasted_iota(jnp.int32, (n_blk, tq), 1)) // SEL_BLOCK
    forced = (jb2 == 0) | (jb2 == cur) | (jb2 == cur - 1)
    val = jnp.where(jb2 > cur, -1e9, jnp.where(forced, 1e9, imp_t))
    bits = pltpu.bitcast(val, jnp.int32)
    key = bits ^ ((bits >> 31) & 0x7FFFFFFF)
    n_causal = (t0 + tq) // SEL_BLOCK
    need = n_causal > n_sel

    def bit_body(i, tau):
        cand = tau | jnp.left_shift(1, 30 - i)
        cnt = jnp.sum(jnp.where(key >= cand, 1, 0), axis=0, keepdims=True)
        return jnp.where(cnt >= n_sel, cand, tau)

    tau = lax.fori_loop(0, jnp.where(need, 31, 0), bit_body, jnp.zeros((1, tq), jnp.int32))
    above = key > tau
    tied = key == tau
    room = n_sel - jnp.sum(jnp.where(above, 1, 0), axis=0, keepdims=True)
    tri = jnp.where(lax.broadcasted_iota(jnp.int32, (n_blk, n_blk), 1) <= lax.broadcasted_iota(jnp.int32, (n_blk, n_blk), 0),
                    1.0, 0.0).astype(BF16)
    prefix = _dot(tri, jnp.where(tied, 1.0, 0.0).astype(BF16))
    chosen = above | (tied & (prefix <= room.astype(F32))) | jnp.logical_not(need)
    neg_t = jnp.where(chosen, 0.0, NEG_INF)
    neg4 = jnp.concatenate([neg_t] * nq, axis=1)
    neg_ref[...] = neg4

    blk0 = iq * bpt
    n_far = jnp.maximum(blk0 - bpt, 0)

    def split_rows(masked, live_row):
        return [jnp.where(masked, neg_parts[i], piece.astype(F32)) for i, piece in enumerate(_split3(live_row))]

    brow = lax.broadcasted_iota(jnp.int32, (n_blk, 1), 0)
    far_parts = split_rows((neg4 < 0.0) | (brow >= n_far), far_row)
    for part in range(3):
        add_ref[part] = far_parts[part].reshape(n_blk // 16, 16, w).astype(BF16)

    near_rows = []
    for j in range(16):
        off = (j - (blk0 - bpt)) % 16
        row = neg_ref[pl.ds(jnp.clip(blk0 - bpt + off, 0, n_blk - 1), 1), :]
        near_rows.append(jnp.where(off < 2 * bpt, row, 0.0))
    near_mask = jnp.concatenate(near_rows, axis=0) < 0.0
    rhs_near = aug_rhs([p.astype(BF16) for p in split_rows(near_mask, jnp.zeros((1, w), F32))])

    near0 = kv_start(t0 - tq)
    edge0 = kv_start(jnp.maximum(t0 - 2 * tq, -tq))
    s_win = _dot(kw_ref[0, 0, pl.ds(near0, 2 * tq), :], rhs_plain) + bn_ref[...]
    s_edge = _dot(kw_ref[0, 0, pl.ds(edge0, tq), :], rhs_plain) + edge_ref[...]
    flash_first(mw_ref, accw_ref, s_win, vwt_ref[0, 0, :, pl.ds(near0, 2 * tq)])
    flash_next(mw_ref, accw_ref, s_edge, jnp.max(s_edge, axis=0, keepdims=True), vwt_ref[0, 0, :, pl.ds(edge0, tq)])
    o_wt = flash_out(accw_ref)

    s_sel = _dot(ks_ref[0, 0, pl.ds(near0, 2 * tq), :], rhs_near) + bn_ref[...]
    flash_first(m_ref, acc_ref, s_sel, vst_ref[0, 0, :, pl.ds(near0, 2 * tq)])

    bpf = tkf // SEL_BLOCK
    n_tiles = n_far // bpf

    def qk_stage(c, s_buf, m_buf):
        chunk = (c * bpf) // 16
        s = _dot(ks_ref[0, 0, pl.ds(kv_start(c * tkf), tkf), :],
                 aug_rhs([add_ref[0, chunk], add_ref[1, chunk], add_ref[2, chunk]]))
        s_buf[...] = s
        m_buf[...] = jnp.max(s, axis=0, keepdims=True)

    def sm_stage(c, s_buf, m_buf):
        flash_next(m_ref, acc_ref, s_buf[...], m_buf[...], vst_ref[0, 0, :, pl.ds(kv_start(c * tkf), tkf)])

    @pl.when(n_tiles > 0)
    def _():
        qk_stage(0, s0_ref, m0_ref)

    def far_body(i, carry):
        qk_stage(2 * i + 1, s1_ref, m1_ref)
        sm_stage(2 * i, s0_ref, m0_ref)
        qk_stage(2 * i + 2, s0_ref, m0_ref)
        sm_stage(2 * i + 1, s1_ref, m1_ref)
        return carry

    lax.fori_loop(0, (n_tiles + 1) // 2, far_body, 0)
    o_st = flash_out(acc_ref)

    sg = jax.nn.sigmoid(gate_ref[0, 0])
    outs = []
    for h in range(nq):
        cols = slice(h * tq, (h + 1) * tq)
        outs.append(sg[3 * h:3 * h + 1] * o_ct[:, cols] + sg[3 * h + 1:3 * h + 2] * o_st[:, cols]
                    + sg[3 * h + 2:3 * h + 3] * o_wt[:, cols])
    o_ref[0] = jnp.concatenate(outs, axis=0).T.astype(o_ref.dtype)


def _nsa_attention(tab, qt, kc, vct, ks, vst, kw, vwt, gate):
    b, _, dh, t = qt.shape
    g = NSA_KV_HEADS
    nq = NSA_QPG
    tq = NSA_TQ
    w = nq * tq
    assert t % tq == 0 and tq >= REL_MAX_DIST and NSA_WINDOW == 2 * tq and tq % SEL_BLOCK == 0
    assert NSA_TKF % SEL_BLOCK == 0 and t >= NSA_TKF and tq % LANE == 0
    assert NSA_NEAR + 1 < NSA_CB_ROWS
    n_cmp = kc.shape[2]
    n_blk = t // SEL_BLOCK
    n_sel = min(SEL_COUNT, n_blk)
    assert n_sel >= 3
    tp = ks.shape[2]
    bpf = NSA_TKF // SEL_BLOCK
    assert tp == t + tq and vst.shape[3] == tp and 16 % bpf == 0 and n_blk % 16 == 0 and tq % NSA_TKF == 0
    assert ks.shape[3] == 2 * dh and vst.shape[2] == dh + 16
    kspec = pl.BlockSpec((1, 1, tp, 2 * dh), lambda j, i, k: (i, j, 0, 0))
    vtspec = pl.BlockSpec((1, 1, dh + 16, tp), lambda j, i, k: (i, j, 0, 0))
    return pl.pallas_call(
        functools.partial(_nsa_attn_kernel, n_sel=n_sel),
        grid=(g, b, t // tq),
        in_specs=[pl.BlockSpec(memory_space=pltpu.SMEM),
                  pl.BlockSpec((1, nq, dh, tq), lambda j, i, k: (i, j, 0, k)),
                  pl.BlockSpec((1, 1, n_cmp, dh), lambda j, i, k: (i, j, 0, 0)),
                  pl.BlockSpec((1, 1, dh, n_cmp), lambda j, i, k: (i, j, 0, 0)),
                  kspec, vtspec, kspec, vtspec,
                  pl.BlockSpec((1, 1, 16, tq), lambda j, i, k: (i, j, 0, k))],
        out_specs=pl.BlockSpec((1, tq, nq * dh), lambda j, i, k: (i, k, j)),
        out_shape=jax.ShapeDtypeStruct((b, t, g * nq * dh), BF16),
        scratch_shapes=[pltpu.VMEM((2 * tq, w), F32),
                        pltpu.VMEM((tq, w), F32),
                        pltpu.VMEM((3, NSA_CB_ROWS, w), BF16),
                        pltpu.VMEM((n_blk, w), F32),
                        pltpu.VMEM((3, n_blk // 16, 16, w), BF16),
                        pltpu.VMEM((1, w), F32),
                        pltpu.VMEM((dh + 16, w), F32),
                        pltpu.VMEM((1, w), F32),
                        pltpu.VMEM((dh + 16, w), F32),
                        pltpu.VMEM((NSA_TKF, w), F32), pltpu.VMEM((NSA_TKF, w), F32),
                        pltpu.VMEM((1, w), F32), pltpu.VMEM((1, w), F32)],
        compiler_params=_cparams(("arbitrary", "arbitrary", "arbitrary")),
        name="nsa_attn",
    )(tab, qt, kc, vct, ks, vst, kw, vwt, gate)


def _nsa_layer(x, w_in, w_out, cmp_pos, cmp_w1, cmp_w2, rel_bias, gain, bias, alpha):
    b, t, d = x.shape
    g, nq, dh = NSA_KV_HEADS, NSA_QPG, HEAD_DIM
    kvw = g * dh
    x2 = x.reshape(b * t, d)
    off = ATTN_HEADS * dh
    cols_t = np.concatenate([np.arange(off), np.arange(off + 3 * kvw, off + 4 * kvw),
                             np.arange(off + 5 * kvw, off + 6 * kvw)])
    cols_k = np.concatenate([np.arange(off + 2 * kvw, off + 3 * kvw), np.arange(off + 4 * kvw, off + 5 * kvw)])
    cols_f = np.concatenate([np.arange(off, off + 2 * kvw), np.arange(off + 6 * kvw, off + 6 * kvw + 3 * ATTN_HEADS)])
    n_pad = -len(cols_f) % LANE
    w_f = jnp.pad(w_in[:, cols_f], ((0, 0), (0, n_pad))).astype(BF16)
    ht = _proj_t(x, w_in[:, cols_t].T.astype(BF16), tn=512)
    hk = _proj(x2, w_in[:, cols_k].astype(BF16), tn=len(cols_k), out_dtype=BF16).reshape(b, t, len(cols_k))
    hf = _proj(x2, w_f, tn=len(cols_f) + n_pad).reshape(b, t, len(cols_f) + n_pad)

    def heads(src, lo, n_heads):
        return src[:, :, lo:lo + n_heads * dh].reshape(b, t, n_heads, dh).transpose(0, 2, 1, 3)

    def heads_t(lo, n_heads):
        return ht[:, lo:lo + n_heads * dh, :].reshape(b, n_heads, dh, t)

    qt = heads_t(0, ATTN_HEADS)
    zk = heads(hf, 0, g).reshape(b, g, t // CMP_STRIDE, CMP_STRIDE * dh)
    zv = heads(hf, kvw, g).reshape(b, g, t // CMP_STRIDE, CMP_STRIDE * dh)
    tp = t + NSA_TQ
    is_pad = np.arange(tp) < NSA_TQ
    slot = (np.arange(tp) - NSA_TQ) // SEL_BLOCK % 16
    onehot = ((np.arange(16)[None, :] == slot[:, None]) & ~is_pad[:, None]).astype(np.float32)
    pad_cols = ((np.arange(16)[None, :] < 3) & is_pad[:, None]).astype(np.float32)
    k_cols = jnp.asarray(np.concatenate([onehot] * 3 + [pad_cols], axis=1), BF16)
    v_rows = jnp.asarray((np.arange(16)[:, None] == 0) * np.ones((1, tp)), BF16)

    def aug_k(z):
        z = jnp.pad(z, ((0, 0), (0, 0), (NSA_TQ, 0), (0, 0)))
        return jnp.concatenate([z, jnp.broadcast_to(k_cols, (b, g, tp, dh))], axis=3)

    def aug_vt(z):
        z = jnp.pad(z, ((0, 0), (0, 0), (0, 0), (NSA_TQ, 0)))
        return jnp.concatenate([z, jnp.broadcast_to(v_rows, (b, g, 16, tp))], axis=2)

    ks = aug_k(heads(hk, 0, g))
    vst = aug_vt(heads_t(off, g))
    kw = aug_k(heads(hk, kvw, g))
    vwt = aug_vt(heads_t(off + kvw, g))
    gate = hf[:, :, 2 * kvw:2 * kvw + 3 * ATTN_HEADS].reshape(b, t, g, 3 * nq)
    gate = jnp.pad(gate.transpose(0, 2, 3, 1), ((0, 0), (0, 0), (0, 16 - 3 * nq), (0, 0)))

    pos = cmp_pos.reshape(2, 2, 1, CMP_STRIDE * dh)
    w1 = cmp_w1.reshape(2, 2, CMP_STRIDE * dh, cmp_w1.shape[-1]).astype(BF16)
    kc, vct = _nsa_compress(zk, zv, pos, w1, cmp_w2.astype(BF16))
    o = _nsa_attention(rel_bias, qt, kc, vct, ks, vst, kw, vwt, gate)
    y = _proj_ln(o.reshape(b * t, d), w_out.astype(BF16), x2, gain, bias, alpha)
    return y.reshape(b, t, d)


def _hgrn_kernel(lbp_ref, gain_ref, zq_ref, zf_ref, zi_ref, zg_ref, o_ref, st_ref, *, layer):
    C = HGRN_TC
    t = zq_ref.shape[1]
    kdim = zq_ref.shape[2]

    p = lbp_ref[...]
    e = jnp.exp(p - jnp.max(p, axis=0, keepdims=True))
    sm = e / jnp.sum(e, axis=0, keepdims=True)
    cs = sm[0:1]
    for r in range(1, layer + 1):
        cs = cs + sm[r:r + 1]
    lb = cs - sm[0:1]
    log_lb = jnp.log(lb)
    log_1m = jnp.log1p(-lb)
    gain = gain_ref[...]

    st_ref[...] = jnp.zeros(st_ref.shape, F32)

    rr = lax.broadcasted_iota(jnp.int32, (C, C), 0)
    cc = lax.broadcasted_iota(jnp.int32, (C, C), 1)
    srow = lax.broadcasted_iota(jnp.int32, (C, 1), 0)
    halves = [1 << i for i in range(C.bit_length() - 1)]
    small = [h for h in halves if h < 8]
    sel_rows = [jnp.where(cc <= rr, 1.0, 0.0).astype(BF16)]
    sel_rows += [jnp.where(cc <= (rr // (2 * h)) * (2 * h) + h - 1, 1.0, 0.0).astype(BF16) for h in small]
    cum_sel = jnp.concatenate(sel_rows, axis=0)
    right = [(srow // h) % 2 == 1 for h in halves]
    same_blk = [rr // (2 * h) == cc // (2 * h) for h in halves]
    eye = rr == cc

    def chunk(c, carry):
        r0 = pl.multiple_of(c * C, C)
        zq = zq_ref[0, pl.ds(r0, C), :]
        zf = zf_ref[0, pl.ds(r0, C), :]
        v = zi_ref[0, pl.ds(r0, C), :]
        zg = zg_ref[0, pl.ds(r0, C), :]
        q = jax.nn.silu(zq)
        log_f = jnp.logaddexp(log_lb, log_1m + jax.nn.log_sigmoid(zf))
        kk = (1.0 - lb) * jax.nn.sigmoid(-zf)
        vb = v.astype(BF16)

        g_hi, g_mid, g_lo = _split3(log_f)
        cums = _dot(cum_sel, g_hi) + _dot(cum_sel, g_mid) + _dot(cum_sel, g_lo)
        bcum = cums[0:C]
        b_last = bcum[C - 1:C, :]

        st = st_ref[...]
        o = _dot_nt((q * jnp.exp(bcum)).astype(BF16), st.astype(BF16))

        a = jnp.where(eye, _dot_nt(q.astype(BF16), kk.astype(BF16)), 0.0)
        for lvl, h in enumerate(halves):
            if h in small:
                b_ref = cums[(lvl + 1) * C:(lvl + 2) * C]
            else:
                b_ref = jnp.broadcast_to(bcum.reshape(C // (2 * h), 2 * h, kdim)[:, h - 1:h, :],
                                         (C // (2 * h), 2 * h, kdim)).reshape(C, kdim)
            e = jnp.exp(-jnp.abs(bcum - b_ref))
            q_side = jnp.where(right[lvl], q * e, 0.0)
            k_side = jnp.where(right[lvl], 0.0, kk * e)
            a = a + jnp.where(same_blk[lvl], _dot_nt(q_side.astype(BF16), k_side.astype(BF16)), 0.0)
        o = o + _dot(a.astype(BF16), vb)

        kd = kk * jnp.exp(b_last - bcum)
        st_ref[...] = st * jnp.exp(b_last) + _dot(v.T.astype(BF16), kd.astype(BF16))

        o = o * lax.rsqrt(jnp.mean(o * o, axis=-1, keepdims=True) + RMS_EPS) * gain
        o = o * jax.nn.silu(zg)
        o_ref[0, pl.ds(r0, C), :] = o.astype(o_ref.dtype)
        return carry

    lax.fori_loop(0, t // C, chunk, 0, unroll=4)


def _hgrn_layer(x, w_in, w_out, norm_gain, lb_param, layer, gain, bias, alpha):
    b, t, d = x.shape
    nh = HGRN_HEADS
    kd = d // nh
    x2 = x.reshape(b * t, d)
    h = _proj(x2, w_in.astype(BF16), tn=1024).reshape(b, t, 4 * d)

    def zspec(part):
        return pl.BlockSpec((1, t, kd), lambda i, j: (i, 0, part * nh + j))

    o = pl.pallas_call(
        functools.partial(_hgrn_kernel, layer=layer),
        grid=(b, nh),
        in_specs=[pl.BlockSpec((lb_param.shape[0], kd), lambda i, j: (0, j)),
                  pl.BlockSpec((1, kd), lambda i, j: (0, 0)),
                  zspec(0), zspec(1), zspec(2), zspec(3)],
        out_specs=pl.BlockSpec((1, t, kd), lambda i, j: (i, 0, j)),
        out_shape=jax.ShapeDtypeStruct((b, t, d), BF16),
        scratch_shapes=[pltpu.VMEM((kd, kd), F32)],
        compiler_params=_cparams(("parallel", "parallel")),
        name="hgrn",
    )(lb_param, norm_gain.reshape(1, kd), h, h, h, h)
    y = _proj_ln(o.reshape(b * t, d), w_out.astype(BF16), x2, gain, bias, alpha)
    return y.reshape(b, t, d)


def _swa_kernel(tab_ref, sink_ref, qt_ref, k_ref, vt_ref, o_ref, bias_ref):
    L = SWA_WINDOW
    nq = SWA_QPG
    dh = HEAD_DIM
    w = nq * L
    g = pl.program_id(1)
    n = pl.program_id(2)

    def head_cols(fn):
        return jnp.concatenate([fn(h) for h in range(nq)], axis=1)

    @pl.when(n == 0)
    def _init():
        jj = lax.broadcasted_iota(jnp.int32, (L, L), 0)
        ii = lax.broadcasted_iota(jnp.int32, (L, L), 1)
        d = ii - jj
        b_cur = _t5_bucket(d)
        b_prev = _t5_bucket(d + L)
        for h in range(nq):
            col = g * nq + h
            cols = slice(h * L, (h + 1) * L)
            bias_ref[0:L, cols] = jnp.where(jj > ii, _bias_lookup(b_prev, tab_ref, col) * LOG2E, NEG_INF)
            bias_ref[L:2 * L, cols] = jnp.where(d >= 0, _bias_lookup(b_cur, tab_ref, col) * LOG2E, NEG_INF)

    qt = (head_cols(lambda h: qt_ref[0, h]).astype(F32) * (dh ** -0.5 * LOG2E)).astype(BF16)
    sink = head_cols(lambda h: jnp.full((1, L), sink_ref[g * nq + h] * LOG2E, F32))
    start = pl.multiple_of(n * L, L)
    s = _dot(k_ref[0, 0, pl.ds(start, 2 * L), :], qt) + bias_ref[...]
    first = jnp.where(n > 0, jnp.zeros((1, w), F32), NEG_INF)
    s = jnp.concatenate([s[0:L] + first, s[L:2 * L]], axis=0)
    m = jnp.maximum(jnp.max(s, axis=0, keepdims=True), sink)
    acc = _dot(vt_ref[0, 0, :, pl.ds(start, 2 * L)], jnp.exp2(s - m).astype(BF16))
    o_t = acc[0:dh] * (1.0 / (acc[dh:dh + 1] + jnp.exp2(sink - m)))
    o_ref[0] = jnp.concatenate([o_t[:, h * L:(h + 1) * L] for h in range(nq)], axis=0).T.astype(o_ref.dtype)


def _swa_layer(x, w_in, w_out, sinks, rel_bias, gain, bias, alpha):
    b, t, d = x.shape
    kvh, nq, dh, L = SWA_KV_HEADS, SWA_QPG, HEAD_DIM, SWA_WINDOW
    assert t % L == 0 and L >= REL_MAX_DIST and L % LANE == 0
    x2 = x.reshape(b * t, d)
    off = ATTN_HEADS * dh
    cols_t = np.concatenate([np.arange(off), np.arange(off + kvh * dh, off + 2 * kvh * dh)])
    ht = _proj_t(x, w_in[:, cols_t].T.astype(BF16), tn=len(cols_t) // 2)
    hk = _proj(x2, w_in[:, off:off + kvh * dh].astype(BF16), tn=kvh * dh, out_dtype=BF16).reshape(b, t, kvh, dh)
    qt = ht[:, 0:off, :].reshape(b, ATTN_HEADS, dh, t)
    tp = t + L
    k = jnp.pad(hk.transpose(0, 2, 1, 3), ((0, 0), (0, 0), (L, 0), (0, 0)))
    ones_rows = jnp.asarray((np.arange(16)[:, None] == 0) * np.ones((1, tp)), BF16)
    vt = jnp.pad(ht[:, off:, :].reshape(b, kvh, dh, t), ((0, 0), (0, 0), (0, 0), (L, 0)))
    vt = jnp.concatenate([vt, jnp.broadcast_to(ones_rows, (b, kvh, 16, tp))], axis=2)
    o = pl.pallas_call(
        _swa_kernel,
        grid=(b, kvh, t // L),
        in_specs=[pl.BlockSpec(memory_space=pltpu.SMEM),
                  pl.BlockSpec(memory_space=pltpu.SMEM),
                  pl.BlockSpec((1, nq, dh, L), lambda i, j, n: (i, j, 0, n)),
                  pl.BlockSpec((1, 1, tp, dh), lambda i, j, n: (i, j, 0, 0)),
                  pl.BlockSpec((1, 1, dh + 16, tp), lambda i, j, n: (i, j, 0, 0))],
        out_specs=pl.BlockSpec((1, L, nq * dh), lambda i, j, n: (i, n, j)),
        out_shape=jax.ShapeDtypeStruct((b, t, d), BF16),
        scratch_shapes=[pltpu.VMEM((2 * L, nq * L), F32)],
        compiler_params=_cparams(("parallel", "parallel", "arbitrary")),
        name="swa_attn",
    )(rel_bias, sinks, qt, k, vt)
    y = _proj_ln(o.reshape(b * t, d), w_out.astype(BF16), x2, gain, bias, alpha)
    return y.reshape(b, t, d)


def kernel(x, rel_bias, ln_gain, ln_bias, ffn1_w_gate, ffn1_w_up, ffn1_w_down, ffn2_w_gate, ffn2_w_up,
           ffn2_w_down, nsa_w_in, nsa_w_out, nsa_cmp_pos, nsa_cmp_w1, nsa_cmp_w2, hgrn_w_in, hgrn_w_out,
           hgrn_norm_gain, hgrn_lb, swa_w_in, swa_w_out, swa_sinks):
    depth = ln_gain.shape[0]
    alpha = (2.0 * depth) ** 0.25
    b, t, d = x.shape

    def ffn(x, wg, wu, wd, gain, bias):
        y = _ffn_ln(x.reshape(b * t, d), wg.astype(BF16), wu.astype(BF16), wd.astype(BF16), gain, bias, alpha)
        return y.reshape(b, t, d)

    for i in range(depth):
        x = ffn(x, ffn1_w_gate[i], ffn1_w_up[i], ffn1_w_down[i], ln_gain[i, 0], ln_bias[i, 0])
        kind, slot = i % N_MIXERS, i // N_MIXERS
        if kind == 0:
            x = _nsa_layer(x, nsa_w_in[slot], nsa_w_out[slot], nsa_cmp_pos[slot], nsa_cmp_w1[slot],
                           nsa_cmp_w2[slot], rel_bias, ln_gain[i, 1], ln_bias[i, 1], alpha)
        elif kind == 1:
            x = _hgrn_layer(x, hgrn_w_in[slot], hgrn_w_out[slot], hgrn_norm_gain[slot], hgrn_lb, i,
                            ln_gain[i, 1], ln_bias[i, 1], alpha)
        else:
            x = _swa_layer(x, swa_w_in[slot], swa_w_out[slot], swa_sinks[slot], rel_bias,
                           ln_gain[i, 1], ln_bias[i, 1], alpha)
        x = ffn(x, ffn2_w_gate[i], ffn2_w_up[i], ffn2_w_down[i], ln_gain[i, 2], ln_bias[i, 2])
    return x
```
